```python
import math
import jax, jax.numpy as jnp
from jax import lax
import numpy as np

D_MODEL = 1024
BATCH = 1
SEQ = 16384
DEPTH = 2

HEAD_DIM = 64
N_ATTN_HEADS = 12
ATTN_WIDTH = N_ATTN_HEADS * HEAD_DIM
N_CONV_GROUPS = 4
CONV_WIDTH = D_MODEL - ATTN_WIDTH
CONV_K = 3
IN_PROJ_WIDTH = 3 * ATTN_WIDTH + 3 * CONV_WIDTH
DILATED_PATTERNS = ((128, 1), (512, 4), (2048, 16))
MAX_SPAN = max(w for w, _ in DILATED_PATTERNS)
Q_BLOCK = 128
ROPE_THETA = 10000.0
EPS = 1e-6
PEER_HEADS = 8
PEER_NKEYS = 128
PEER_N_EXPERTS = PEER_NKEYS * PEER_NKEYS
PEER_DKEY = 256
PEER_TOPK = 16
TOKEN_BLOCK = 128
N_MOD = 6

kernel_name = "hybrid_dilated_attn_shortconv_peer"


def _rms(x, g):
    xf = x.astype(jnp.float32)
    y = xf * lax.rsqrt(jnp.mean(xf * xf, axis=-1, keepdims=True) + EPS)
    return (y * g.astype(jnp.float32)).astype(x.dtype)


def _rope(x, positions):
    half = HEAD_DIM // 2
    freq = ROPE_THETA ** (-jnp.arange(half, dtype=jnp.float32) / half)
    ang = positions.astype(jnp.float32)[..., None] * freq
    cos = jnp.cos(ang)[:, :, None, :]
    sin = jnp.sin(ang)[:, :, None, :]
    xf = x.astype(jnp.float32)
    x1, x2 = xf[..., :half], xf[..., half:]
    return jnp.concatenate([x1 * cos - x2 * sin, x2 * cos + x1 * sin], axis=-1).astype(x.dtype)


def _dilated_attention(q, k, v):
    B, S, H, D = q.shape
    scale = D ** -0.5
    n_blk = S // Q_BLOCK
    dists = np.concatenate([np.arange(w // d + 1) * d for w, d in DILATED_PATTERNS]).astype(np.int32)
    rel = (np.arange(Q_BLOCK, dtype=np.int32)[:, None] - dists[None, :])
    local = jnp.asarray(rel + MAX_SPAN)
    rel_j = jnp.asarray(rel)
    kp = jnp.pad(k, ((0, 0), (MAX_SPAN, 0), (0, 0), (0, 0)))
    vp = jnp.pad(v, ((0, 0), (MAX_SPAN, 0), (0, 0), (0, 0)))
    qb = q.reshape(B, n_blk, Q_BLOCK, H, D).transpose(1, 0, 2, 3, 4)

    def block(args):
        i, qblk = args
        t0 = i * Q_BLOCK
        kw = lax.dynamic_slice_in_dim(kp, t0, Q_BLOCK + MAX_SPAN, axis=1)
        vw = lax.dynamic_slice_in_dim(vp, t0, Q_BLOCK + MAX_SPAN, axis=1)
        kg = jnp.take(kw, local, axis=1)
        vg = jnp.take(vw, local, axis=1)
        logits = jnp.einsum('bqhd,bqkhd->bhqk', qblk.astype(jnp.float32), kg.astype(jnp.float32)) * scale
        valid = (t0 + rel_j) >= 0
        logits = jnp.where(valid[None, None], logits, -jnp.inf)
        p = jax.nn.softmax(logits, axis=-1)
        out = jnp.einsum('bhqk,bqkhd->bqhd', p, vg.astype(jnp.float32))
        return out.astype(q.dtype)

    out = lax.map(block, (jnp.arange(n_blk, dtype=jnp.int32), qb))
    return out.transpose(1, 0, 2, 3, 4).reshape(B, S, H, D)


def _short_conv(bg, cg, xv, w_conv):
    S = xv.shape[1]
    u = cg * xv
    up = jnp.pad(u, ((0, 0), (CONV_K - 1, 0), (0, 0)))
    y = up[:, 0:S] * w_conv[0]
    for kk in range(1, CONV_K):
        y = y + up[:, kk:kk + S] * w_conv[kk]
    return bg * y


def _peer(h, w_q, sub_keys, u_tab, v_tab):
    B, S, D = h.shape
    q = (h @ w_q).reshape(B, S, PEER_HEADS, 2, PEER_DKEY // 2)
    scores = jnp.einsum('bshpc,hpnc->bshpn', q.astype(jnp.float32), sub_keys.astype(jnp.float32))
    s_top, i_top = lax.top_k(scores, PEER_TOPK)
    cand = s_top[..., 0, :, None] + s_top[..., 1, None, :]
    cand_idx = i_top[..., 0, :, None] * PEER_NKEYS + i_top[..., 1, None, :]
    cand = cand.reshape(B, S, PEER_HEADS, PEER_TOPK * PEER_TOPK)
    cand_idx = cand_idx.reshape(B, S, PEER_HEADS, PEER_TOPK * PEER_TOPK)
    best, pos = lax.top_k(cand, PEER_TOPK)
    experts = jnp.take_along_axis(cand_idx, pos, axis=-1)
    gates = jax.nn.softmax(best, axis=-1).astype(h.dtype)
    T = B * S
    n_blk = T // TOKEN_BLOCK
    hb = h.reshape(n_blk, TOKEN_BLOCK, D)
    eb = experts.reshape(n_blk, TOKEN_BLOCK, PEER_HEADS, PEER_TOPK)
    gb = gates.reshape(n_blk, TOKEN_BLOCK, PEER_HEADS, PEER_TOPK)

    def block(args):
        hx, e, g = args
        ue = jnp.take(u_tab, e, axis=0)
        act = jax.nn.gelu(jnp.einsum('td,thkd->thk', hx, ue), approximate=False)
        ve = jnp.take(v_tab, e, axis=0)
        return jnp.einsum('thk,thkd->td', g * act, ve)

    out = lax.map(block, (hb, eb, gb))
    return out.reshape(B, S, D)


def _layer(x, c, positions, w_ada, b_ada, norm_mix, norm_ffn, w_in, q_norm, k_norm,
           conv_w, w_out, peer_wq, peer_keys, peer_u, peer_v):
    B, S, _ = x.shape
    mod = (c @ w_ada + b_ada)[:, None, :]
    sh1, sc1, gt1, sh2, sc2, gt2 = jnp.split(mod, N_MOD, axis=-1)
    h = _rms(x, norm_mix) * (1 + sc1) + sh1
    proj = h @ w_in
    A, C = ATTN_WIDTH, CONV_WIDTH
    q, k, v, bg, cg, xv = jnp.split(proj, [A, 2 * A, 3 * A, 3 * A + C, 3 * A + 2 * C], axis=-1)
    q = _rope(_rms(q.reshape(B, S, N_ATTN_HEADS, HEAD_DIM), q_norm), positions)
    k = _rope(_rms(k.reshape(B, S, N_ATTN_HEADS, HEAD_DIM), k_norm), positions)
    v = v.reshape(B, S, N_ATTN_HEADS, HEAD_DIM)
    attn = _dilated_attention(q, k, v).reshape(B, S, ATTN_WIDTH)
    conv = _short_conv(bg, cg, xv, conv_w)
    mix = jnp.concatenate([attn, conv], axis=-1) @ w_out
    x = x + gt1 * mix
    h = _rms(x, norm_ffn) * (1 + sc2) + sh2
    x = x + gt2 * _peer(h, peer_wq, peer_keys, peer_u, peer_v)
    return x


def setup_inputs(seed: int = 0) -> dict:
    key = jax.random.key(seed)
    ks = jax.random.split(key, 16)
    f32 = jnp.float32
    D = D_MODEL
    x = jax.random.normal(ks[0], (BATCH, SEQ, D), f32)
    c = jax.random.normal(ks[1], (BATCH, D), f32)
    positions = jnp.broadcast_to(jnp.arange(SEQ, dtype=jnp.int32)[None, :], (BATCH, SEQ))
    w_ada = jax.random.normal(ks[2], (DEPTH, D, N_MOD * D), f32) * (0.1 * D ** -0.5)
    b_ada = jax.random.normal(ks[3], (DEPTH, N_MOD * D), f32) * 0.02
    norm_mix = 1.0 + 0.02 * jax.random.normal(ks[4], (DEPTH, D), f32)
    norm_ffn = 1.0 + 0.02 * jax.random.normal(ks[5], (DEPTH, D), f32)
    w_in = jax.random.normal(ks[6], (DEPTH, D, IN_PROJ_WIDTH), f32) * D ** -0.5
    q_norm = 1.0 + 0.02 * jax.random.normal(ks[7], (DEPTH, HEAD_DIM), f32)
    k_norm = 1.0 + 0.02 * jax.random.normal(ks[8], (DEPTH, HEAD_DIM), f32)
    conv_w = jax.random.normal(ks[9], (DEPTH, CONV_K, CONV_WIDTH), f32) * CONV_K ** -0.5
    w_out = jax.random.normal(ks[10], (DEPTH, D, D), f32) * D ** -0.5
    peer_wq = jax.random.normal(ks[11], (DEPTH, D, PEER_HEADS * PEER_DKEY), f32) * D ** -0.5
    peer_keys = jax.random.normal(ks[12], (DEPTH, PEER_HEADS, 2, PEER_NKEYS, PEER_DKEY // 2), f32) * (PEER_DKEY // 2) ** -0.5
    peer_u = jax.random.normal(ks[13], (DEPTH, PEER_N_EXPERTS, D), f32) * D ** -0.5
    peer_v = jax.random.normal(ks[14], (DEPTH, PEER_N_EXPERTS, D), f32) * PEER_HEADS ** -0.5
    return {"x": x, "c": c, "positions": positions, "w_ada": w_ada, "b_ada": b_ada,
            "norm_mix": norm_mix, "norm_ffn": norm_ffn, "w_in": w_in, "q_norm": q_norm,
            "k_norm": k_norm, "conv_w": conv_w, "w_out": w_out, "peer_wq": peer_wq,
            "peer_keys": peer_keys, "peer_u": peer_u, "peer_v": peer_v}


def reference(x, c, positions, w_ada, b_ada, norm_mix, norm_ffn, w_in, q_norm, k_norm,
              conv_w, w_out, peer_wq, peer_keys, peer_u, peer_v):
    for l in range(DEPTH):
        x = _layer(x, c, positions, w_ada[l], b_ada[l], norm_mix[l], norm_ffn[l], w_in[l],
                   q_norm[l], k_norm[l], conv_w[l], w_out[l], peer_wq[l], peer_keys[l],
                   peer_u[l], peer_v[l])
    return x
```

```python
import functools

import jax
import jax.numpy as jnp
from jax import lax
from jax.experimental import pallas as pl
from jax.experimental.pallas import tpu as pltpu

F32 = jnp.float32
BF16 = jnp.bfloat16

D_MODEL = 1024
HEAD_DIM = 64
N_ATTN_HEADS = 12
ATTN_WIDTH = N_ATTN_HEADS * HEAD_DIM
CONV_WIDTH = D_MODEL - ATTN_WIDTH
IN_PROJ_WIDTH = 3 * ATTN_WIDTH + 3 * CONV_WIDTH
BRANCH_KEYS = 128
DILATIONS = (16, 4, 1)
ROPE_THETA = 10000.0
EPS = 1e-6
PEER_HEADS = 8
PEER_NKEYS = 128
PEER_TOPK = 16
N_MOD = 6

LANES = 128
SUBLANES = 8
VMEM_LIMIT = 56 * 1024 * 1024

NEG_INF = float("-inf")


def _dot(a, b):
    return jnp.dot(a, b, preferred_element_type=F32)


def _dot_nt(a, b):
    return lax.dot_general(a, b, (((1,), (1,)), ((), ())), preferred_element_type=F32)


def _gelu(x):
    return 0.5 * x * (1.0 + lax.erf(x * (0.5 ** 0.5)))


def _params(*sem):
    return pltpu.CompilerParams(dimension_semantics=sem, vmem_limit_bytes=VMEM_LIMIT)


def _ada_kernel(c_ref, w_ref, b_ref, o_ref):
    o_ref[...] = jnp.dot(c_ref[...], w_ref[...], preferred_element_type=F32,
                         precision=lax.Precision.HIGHEST) + b_ref[...]


def _ada(c, w_ada, b_ada):
    depth = w_ada.shape[0]
    c8 = jnp.broadcast_to(c, (SUBLANES, D_MODEL))
    out = pl.pallas_call(
        _ada_kernel,
        grid=(depth, N_MOD),
        in_specs=[pl.BlockSpec((SUBLANES, D_MODEL), lambda l, j: (0, 0)),
                  pl.BlockSpec((None, D_MODEL, D_MODEL), lambda l, j: (l, 0, j)),
                  pl.BlockSpec((None, 1, D_MODEL), lambda l, j: (l, 0, j))],
        out_specs=pl.BlockSpec((None, SUBLANES, D_MODEL), lambda l, j: (l, 0, j)),
        out_shape=jax.ShapeDtypeStruct((depth, SUBLANES, N_MOD * D_MODEL), F32),
        compiler_params=_params("parallel", "parallel"),
        name="ada_mod",
    )(c8, w_ada, b_ada.reshape(depth, 1, N_MOD * D_MODEL))
    return out[:, 0, :]


def _rope_kernel(pos_ref, freq_ref, sign_ref, cos_ref, sin_ref):
    ang = pos_ref[...].astype(F32) * freq_ref[...]
    cos_ref[...] = jnp.cos(ang)
    sin_ref[...] = jnp.sin(ang) * sign_ref[...]


def _rope_tables(positions, tm):
    s = positions.shape[-1]
    half = HEAD_DIM // 2
    freq = ROPE_THETA ** (-jnp.arange(half, dtype=F32) / half)
    freq = jnp.tile(freq, LANES // half).reshape(1, LANES)
    sign = jnp.tile(jnp.concatenate([-jnp.ones((half,), F32), jnp.ones((half,), F32)]), LANES // HEAD_DIM)
    sign = sign.reshape(1, LANES)
    row = pl.BlockSpec((1, LANES), lambda i: (0, 0))
    return pl.pallas_call(
        _rope_kernel,
        grid=(s // tm,),
        in_specs=[pl.BlockSpec((tm, 1), lambda i: (i, 0)), row, row],
        out_specs=[pl.BlockSpec((tm, LANES), lambda i: (i, 0))] * 2,
        out_shape=[jax.ShapeDtypeStruct((s, LANES), F32)] * 2,
        compiler_params=_params("parallel"),
        name="rope_tables",
    )(positions.reshape(s, 1), freq, sign)


def _rms_mod(x, g, sc, sh):
    ms = jnp.mean(x * x, axis=-1, keepdims=True)
    y = x * lax.rsqrt(ms + EPS) * g
    return y * (1.0 + sc) + sh


def _inproj_kernel(x_ref, g_ref, sc_ref, sh_ref, w_ref, qg_ref, kg_ref, cos_ref, sin_ref, bd_ref, cw_ref,
                   q_ref, k_ref, v_ref, conv_ref, ubuf_ref, *, tm):
    i = pl.program_id(0)
    h = _rms_mod(x_ref[...], g_ref[...], sc_ref[...], sh_ref[...])
    proj = _dot(h.astype(BF16), w_ref[...])
    cos = cos_ref[...]
    sin = sin_ref[...]
    lane = lax.broadcasted_iota(jnp.int32, (tm, LANES), 1)
    upper = (lane & (HEAD_DIM // 2)) != 0

    def head_norm_rope(z, gain_ref, out_ref, scale):
        for j in range(ATTN_WIDTH // 256):
            zj = z[:, 256 * j:256 * (j + 1)]
            ss = _dot((zj * zj).astype(BF16), bd_ref[...])
            zn = zj * lax.rsqrt(ss * (1.0 / HEAD_DIM) + EPS) * gain_ref[:, 256 * j:256 * (j + 1)]
            for t in range(2):
                zt = zn[:, LANES * t:LANES * (t + 1)]
                partner = jnp.where(upper, pltpu.roll(zt, HEAD_DIM // 2, 1),
                                    pltpu.roll(zt, LANES - HEAD_DIM // 2, 1))
                r = zt * cos + partner * sin
                c0 = 256 * j + LANES * t
                out_ref[:, c0:c0 + LANES] = (r * scale).astype(BF16)

    a = ATTN_WIDTH
    head_norm_rope(proj[:, 0:a], qg_ref, q_ref, HEAD_DIM ** -0.5)
    head_norm_rope(proj[:, a:2 * a], kg_ref, k_ref, 1.0)
    v_ref[...] = proj[:, 2 * a:3 * a].astype(BF16)

    c = CONV_WIDTH
    bg = proj[:, 3 * a:3 * a + c]
    cg = proj[:, 3 * a + c:3 * a + 2 * c]
    xv = proj[:, 3 * a + 2 * c:3 * a + 3 * c]
    u = cg * xv

    @pl.when(i == 0)
    def _():
        ubuf_ref[0:SUBLANES, :] = jnp.zeros((SUBLANES, c), F32)

    ubuf_ref[SUBLANES:SUBLANES + tm, :] = u
    u1 = ubuf_ref[SUBLANES - 1:SUBLANES - 1 + tm, :]
    u2 = ubuf_ref[SUBLANES - 2:SUBLANES - 2 + tm, :]
    y = u2 * cw_ref[0:1, :] + u1 * cw_ref[1:2, :] + u * cw_ref[2:3, :]
    conv_ref[...] = (bg * y).astype(BF16)
    ubuf_ref[0:SUBLANES, :] = ubuf_ref[tm:tm + SUBLANES, :]


def _inproj(x, g, sc, sh, w_in, qg, kg, cos_t, sin_t, bd, conv_w, tm):
    s = x.shape[0]
    row = lambda n: pl.BlockSpec((1, n), lambda i: (0, 0))
    tok = lambda n: pl.BlockSpec((tm, n), lambda i: (i, 0))
    return pl.pallas_call(
        functools.partial(_inproj_kernel, tm=tm),
        grid=(s // tm,),
        in_specs=[tok(D_MODEL), row(D_MODEL), row(D_MODEL), row(D_MODEL),
                  pl.BlockSpec((D_MODEL, IN_PROJ_WIDTH), lambda i: (0, 0)),
                  row(ATTN_WIDTH), row(ATTN_WIDTH), tok(LANES), tok(LANES),
                  pl.BlockSpec((256, 256), lambda i: (0, 0)),
                  pl.BlockSpec((3, CONV_WIDTH), lambda i: (0, 0))],
        out_specs=[tok(ATTN_WIDTH), tok(ATTN_WIDTH), tok(ATTN_WIDTH), tok(CONV_WIDTH)],
        out_shape=[jax.ShapeDtypeStruct((s, ATTN_WIDTH), BF16)] * 3
        + [jax.ShapeDtypeStruct((s, CONV_WIDTH), BF16)],
        scratch_shapes=[pltpu.VMEM((tm + SUBLANES, CONV_WIDTH), F32)],
        compiler_params=_params("arbitrary"),
        name="inproj",
    )(x, g, sc, sh, w_in, qg, kg, cos_t, sin_t, bd, conv_w)


def _attn_kernel(*refs, first, last):
    tq = BRANCH_KEYS
    if first:
        q_ref, kp_ref, kc_ref, vp_ref, vc_ref, acc_out, ml_out = refs
    elif last:
        q_ref, kp_ref, kc_ref, vp_ref, vc_ref, acc_in, ml_in, out_ref = refs
    else:
        q_ref, kp_ref, kc_ref, vp_ref, vc_ref, acc_in, ml_in, acc_out, ml_out = refs
    ib = pl.program_id(1)
    qi = lax.broadcasted_iota(jnp.int32, (tq, tq), 0)
    ki = lax.broadcasted_iota(jnp.int32, (tq, tq), 1)
    mask_c = ki <= qi
    mask_p = jnp.logical_and(ki >= qi, ib > 0)
    lane = ki
    ml_old = None if first else ml_in[...]
    ml_new = jnp.zeros((tq, LANES), F32)
    for h in range(N_ATTN_HEADS):
        sl = slice(HEAD_DIM * h, HEAD_DIM * (h + 1))
        qh = q_ref[:, sl]
        sp = jnp.where(mask_p, _dot_nt(qh, kp_ref[:, sl]), NEG_INF)
        sc = jnp.where(mask_c, _dot_nt(qh, kc_ref[:, sl]), NEG_INF)
        m_cur = jnp.maximum(jnp.max(sp, axis=1, keepdims=True), jnp.max(sc, axis=1, keepdims=True))
        if first:
            m_new = m_cur
        else:
            m_old = jnp.max(jnp.where(lane == h, ml_old, NEG_INF), axis=1, keepdims=True)
            l_old = jnp.sum(jnp.where(lane == 16 + h, ml_old, 0.0), axis=1, keepdims=True)
            m_new = jnp.maximum(m_old, m_cur)
            alpha = jnp.exp(m_old - m_new)
        pp = jnp.exp(sp - m_new)
        pc = jnp.exp(sc - m_new)
        l_cur = jnp.sum(pp, axis=1, keepdims=True) + jnp.sum(pc, axis=1, keepdims=True)
        pv = _dot(pp.astype(BF16), vp_ref[:, sl]) + _dot(pc.astype(BF16), vc_ref[:, sl])
        if first:
            l_new, acc = l_cur, pv
        else:
            l_new = alpha * l_old + l_cur
            acc = alpha * acc_in[:, sl] + pv
        if last:
            out_ref[:, sl] = (acc / l_new).astype(BF16)
        else:
            acc_out[:, sl] = acc
            ml_new = jnp.where(lane == h, m_new, ml_new)
            ml_new = jnp.where(lane == 16 + h, l_new, ml_new)
    if not last:
        ml_out[...] = ml_new


def _attn_pass(q, k, v, state, dil, first, last):
    s = q.shape[0]
    n = s // dil
    tq = BRANCH_KEYS
    view = lambda a: a.reshape(n, dil * a.shape[1])
    cur = lambda w: pl.BlockSpec((tq, w), lambda r, ib: (ib, r))
    prev = lambda w: pl.BlockSpec((tq, w), lambda r, ib: (jnp.maximum(ib - 1, 0), r))
    aw = ATTN_WIDTH
    in_specs = [cur(aw), prev(aw), cur(aw), prev(aw), cur(aw)]
    args = [view(q), view(k), view(k), view(v), view(v)]
    if not first:
        in_specs += [cur(aw), cur(LANES)]
        args += [view(state[0]), view(state[1])]
    if last:
        out_specs = [cur(aw)]
        out_shape = [jax.ShapeDtypeStruct((n, dil * aw), BF16)]
    else:
        out_specs = [cur(aw), cur(LANES)]
        out_shape = [jax.ShapeDtypeStruct((n, dil * aw), F32), jax.ShapeDtypeStruct((n, dil * LANES), F32)]
    outs = pl.pallas_call(
        functools.partial(_attn_kernel, first=first, last=last),
        grid=(dil, n // tq),
        in_specs=in_specs, out_specs=out_specs, out_shape=out_shape,
        compiler_params=_params("parallel", "parallel"),
        name=f"attn_d{dil}",
    )(*args)
    return [o.reshape(s, -1) for o in outs]


def _attention(q, k, v):
    state = None
    for idx, dil in enumerate(DILATIONS):
        state = _attn_pass(q, k, v, state, dil, first=idx == 0, last=idx == len(DILATIONS) - 1)
    return state[0]


def _outproj_kernel(x_ref, attn_ref, conv_ref, woa_ref, woc_ref, gt_ref, g_ref, sc_ref, sh_ref, wq_ref,
                    keys_ref, x1_ref, h2_ref, st_ref):
    mix = _dot(attn_ref[...], woa_ref[...]) + _dot(conv_ref[...], woc_ref[...])
    x1 = x_ref[...] + gt_ref[...] * mix
    x1_ref[...] = x1
    h2 = _rms_mod(x1, g_ref[...], sc_ref[...], sh_ref[...]).astype(BF16)
    h2_ref[...] = h2
    qp = _dot(h2, wq_ref[...])
    for g in range(2 * PEER_HEADS):
        qg = qp[:, PEER_NKEYS * g:PEER_NKEYS * (g + 1)].astype(BF16)
        st_ref[PEER_NKEYS * g:PEER_NKEYS * (g + 1), :] = _dot_nt(keys_ref[g], qg)


def _outproj(x, attn, conv, wo_a, wo_c, gt, g, sc, sh, wq, keys, tm):
    s = x.shape[0]
    nq = wq.shape[1]
    row = lambda n: pl.BlockSpec((1, n), lambda i: (0, 0))
    tok = lambda n: pl.BlockSpec((tm, n), lambda i: (i, 0))
    full = lambda a: pl.BlockSpec(a.shape, lambda i: (0,) * a.ndim)
    return pl.pallas_call(
        _outproj_kernel,
        grid=(s // tm,),
        in_specs=[tok(D_MODEL), tok(ATTN_WIDTH), tok(CONV_WIDTH), full(wo_a), full(wo_c),
                  row(D_MODEL), row(D_MODEL), row(D_MODEL), row(D_MODEL), full(wq), full(keys)],
        out_specs=[tok(D_MODEL), tok(D_MODEL), pl.BlockSpec((nq, tm), lambda i: (0, i))],
        out_shape=[jax.ShapeDtypeStruct((s, D_MODEL), F32), jax.ShapeDtypeStruct((s, D_MODEL), BF16),
                   jax.ShapeDtypeStruct((nq, s), F32)],
        compiler_params=_params("parallel"),
        name="outproj",
    )(x, attn, conv, wo_a, wo_c, gt, g, sc, sh, wq, keys)


def _top16(s, tb):
    n = s.shape[0]
    row = lax.broadcasted_iota(jnp.int32, (n, tb), 0)
    row16 = lax.broadcasted_iota(jnp.int32, (PEER_TOPK, tb), 0)
    cur = s
    rank = jnp.full((n, tb), PEER_TOPK, jnp.int32)
    vals = jnp.zeros((PEER_TOPK, tb), F32)
    for kk in range(PEER_TOPK):
        m = jnp.max(cur, axis=0, keepdims=True)
        idx = jnp.min(jnp.where(cur == m, row, n), axis=0, keepdims=True)
        sel = row == idx
        rank = jnp.where(sel, kk, rank)
        cur = jnp.where(sel, NEG_INF, cur)
        vals = jnp.where(row16 == kk, m, vals)
    return vals, rank


def _topk_kernel(st_ref, rank2_ref, cnt_ref, e1_ref, e2_ref, *, tb):
    kk = PEER_TOPK
    s1 = st_ref[0:PEER_NKEYS, :]
    s2 = st_ref[PEER_NKEYS:2 * PEER_NKEYS, :]
    v1, rank1 = _top16(s1, tb)
    v2, rank2 = _top16(s2, tb)

    sub = lax.broadcasted_iota(jnp.int32, (SUBLANES, tb), 0)
    v2lo, v2hi = v2[0:SUBLANES, :], v2[SUBLANES:kk, :]
    groups = [(v1[0:1, :] + v2lo, sub, 0), (v1[0:1, :] + v2hi, sub + SUBLANES, 0),
              (v1[1:2, :] + v2lo, sub + kk, 1)]
    for a in range(2, SUBLANES):
        nb = kk // (a + 1)
        groups.append((jnp.where(sub < nb, v1[a:a + 1, :] + v2lo, NEG_INF), sub + kk * a, a))
    groups.append((v1[SUBLANES:kk, :] + v2[0:1, :], (sub + SUBLANES) * kk, None))
    cand = [g[0] for g in groups]
    flat = [g[1] for g in groups]
    cur = list(cand)
    big = kk * kk
    for _ in range(kk):
        m = functools.reduce(jnp.maximum, cur)
        m = jnp.max(m, axis=0, keepdims=True)
        hit = [jnp.where(c == m, f, big) for c, f in zip(cur, flat)]
        idx = jnp.min(functools.reduce(jnp.minimum, hit), axis=0, keepdims=True)
        cur = [jnp.where(f == idx, NEG_INF, c) for c, f in zip(cur, flat)]
    sel = [jnp.logical_and(c == NEG_INF, o != NEG_INF) for c, o in zip(cur, cand)]

    one = lambda mk: jnp.where(mk, 1.0, 0.0)
    counts = [jnp.sum(one(sel[0]) + one(sel[1]), axis=0, keepdims=True)]
    for gi in range(2, 2 + SUBLANES - 1):
        counts.append(jnp.sum(one(sel[gi]), axis=0, keepdims=True))
    tail = one(sel[-1])
    cnt = jnp.zeros((PEER_NKEYS, tb), F32)
    for a in range(kk):
        na = counts[a] if a < SUBLANES else tail[a - SUBLANES:a - SUBLANES + 1, :]
        cnt = jnp.where(rank1 == a, na, cnt)

    m1, m2 = v1[0:1, :], v2[0:1, :]
    mt = m1 + m2
    z = functools.reduce(lambda x, y: x + y,
                         [jnp.sum(jnp.where(sl, jnp.exp(o - mt), 0.0), axis=0, keepdims=True)
                          for sl, o in zip(sel, cand)])
    rank2_ref[...] = rank2.astype(F32)
    cnt_ref[...] = cnt
    e1_ref[...] = jnp.exp(s1 - m1)
    e2_ref[...] = jnp.exp(s2 - m2) / z


def _topk(st, tb):
    s = st.shape[1]
    plane = pl.BlockSpec((None, PEER_NKEYS, tb), lambda h, i: (h, 0, i))
    shape = jax.ShapeDtypeStruct((PEER_HEADS, PEER_NKEYS, s), F32)
    return pl.pallas_call(
        functools.partial(_topk_kernel, tb=tb),
        grid=(PEER_HEADS, s // tb),
        in_specs=[pl.BlockSpec((2 * PEER_NKEYS, tb), lambda h, i: (h, i))],
        out_specs=[plane] * 4,
        out_shape=[shape] * 4,
        compiler_params=_params("parallel", "parallel"),
        name="peer_topk",
    )(st)


def _peer_kernel(h_ref, u_ref, vt_ref, rank2_ref, cnt_ref, e1_ref, e2_ref, x1_ref, gt_ref,
                 out_ref, acc_ref, p_ref, *, tm, ec):
    j = pl.program_id(1)
    rows = ec // PEER_NKEYS

    @pl.when(j == 0)
    def _():
        acc_ref[...] = jnp.zeros_like(acc_ref)

    zt = _dot_nt(u_ref[...], h_ref[...])
    crows = pl.ds(pl.multiple_of(j * rows, SUBLANES), rows)
    for t in range(tm // LANES):
        tl = slice(LANES * t, LANES * (t + 1))
        cnt_rows = [cnt_ref[hh, crows, tl] for hh in range(PEER_HEADS)]
        e1_rows = [e1_ref[hh, crows, tl] for hh in range(PEER_HEADS)]
        for r in range(rows):
            w = jnp.zeros((PEER_NKEYS, LANES), F32)
            for hh in range(PEER_HEADS):
                keep = rank2_ref[hh, :, tl] < cnt_rows[hh][r:r + 1, :]
                w = w + jnp.where(keep, e2_ref[hh, :, tl], 0.0) * e1_rows[hh][r:r + 1, :]
            act = _gelu(zt[PEER_NKEYS * r:PEER_NKEYS * (r + 1), tl])
            p_ref[PEER_NKEYS * r:PEER_NKEYS * (r + 1), tl] = (w * act).astype(BF16)
    acc_ref[...] += _dot(vt_ref[...], p_ref[...])

    @pl.when(j == pl.num_programs(1) - 1)
    def _():
        out_ref[...] = x1_ref[...] + gt_ref[...] * acc_ref[...].T


def _peer(h2, u, vt, planes, x1, gt, tm, ec):
    s = h2.shape[0]
    ne = u.shape[0]
    assert ec == SUBLANES * PEER_NKEYS, ec
    plane = pl.BlockSpec((PEER_HEADS, PEER_NKEYS, tm), lambda i, j: (0, 0, i))
    tok = pl.BlockSpec((tm, D_MODEL), lambda i, j: (i, 0))
    return pl.pallas_call(
        functools.partial(_peer_kernel, tm=tm, ec=ec),
        grid=(s // tm, ne // ec),
        in_specs=[tok, pl.BlockSpec((ec, D_MODEL), lambda i, j: (j, 0)),
                  pl.BlockSpec((D_MODEL, ec), lambda i, j: (0, j)),
                  plane, plane, plane, plane, tok, pl.BlockSpec((1, D_MODEL), lambda i, j: (0, 0))],
        out_specs=tok,
        out_shape=jax.ShapeDtypeStruct((s, D_MODEL), F32),
        scratch_shapes=[pltpu.VMEM((D_MODEL, tm), F32), pltpu.VMEM((ec, tm), BF16)],
        compiler_params=_params("parallel", "arbitrary"),
        name="peer_dense",
    )(h2, u, vt, *planes, x1, gt)


def _tile(s, want):
    t = min(want, s)
    assert s % t == 0, (s, t)
    return t


def _layer(x, mod, cos_t, sin_t, bd, norm_mix, norm_ffn, w_in, q_norm, k_norm, conv_w, w_out,
           peer_wq, peer_keys, peer_u, peer_v):
    s = x.shape[0]
    sh1, sc1, gt1, sh2, sc2, gt2 = [m.reshape(1, D_MODEL) for m in jnp.split(mod, N_MOD)]
    tm = _tile(s, 512)
    qg = jnp.tile(q_norm, N_ATTN_HEADS).reshape(1, ATTN_WIDTH)
    kg = jnp.tile(k_norm, N_ATTN_HEADS).reshape(1, ATTN_WIDTH)
    q, k, v, conv = _inproj(x, norm_mix.reshape(1, D_MODEL), sc1, sh1, w_in.astype(BF16), qg, kg,
                            cos_t, sin_t, bd, conv_w, tm)
    attn = _attention(q, k, v)
    wo = w_out.astype(BF16)
    keys = peer_keys.reshape(2 * PEER_HEADS, PEER_NKEYS, -1).astype(BF16)
    x1, h2, st = _outproj(x, attn, conv, wo[:ATTN_WIDTH], wo[ATTN_WIDTH:], gt1,
                          norm_ffn.reshape(1, D_MODEL), sc2, sh2, peer_wq.astype(BF16), keys, tm)
    planes = _topk(st, _tile(s, 256))
    return _peer(h2, peer_u.astype(BF16), peer_v.T.astype(BF16), planes, x1, gt2,
                 _tile(s, 512), SUBLANES * PEER_NKEYS)


def kernel(x, c, positions, w_ada, b_ada, norm_mix, norm_ffn, w_in, q_norm, k_norm, conv_w, w_out,
           peer_wq, peer_keys, peer_u, peer_v):
    b, s, d = x.shape
    assert b == 1 and d == D_MODEL and s % (max(DILATIONS) * BRANCH_KEYS) == 0, x.shape
    depth = w_ada.shape[0]
    mod = _ada(c, w_ada, b_ada)
    cos_t, sin_t = _rope_tables(positions, _tile(s, 1024))
    blk = jnp.arange(256) // HEAD_DIM
    bd = (blk[:, None] == blk[None, :]).astype(BF16)
    y = x.reshape(s, d)
    for l in range(depth):
        y = _layer(y, mod[l], cos_t, sin_t, bd, norm_mix[l], norm_ffn[l], w_in[l], q_norm[l], k_norm[l],
                   conv_w[l], w_out[l], peer_wq[l], peer_keys[l], peer_u[l], peer_v[l])
    return y.reshape(b, s, d)
```

```python
import functools

import jax
import jax.numpy as jnp
from jax import lax
from jax.experimental import pallas as pl
from jax.experimental.pallas import tpu as pltpu

F32 = jnp.float32
BF16 = jnp.bfloat16

D_MODEL = 1024
HEAD_DIM = 64
N_ATTN_HEADS = 12
ATTN_WIDTH = N_ATTN_HEADS * HEAD_DIM
CONV_WIDTH = D_MODEL - ATTN_WIDTH
IN_PROJ_WIDTH = 3 * ATTN_WIDTH + 3 * CONV_WIDTH
BRANCH_KEYS = 128
DILATIONS = (16, 4, 1)
ROPE_THETA = 10000.0
EPS = 1e-6
PEER_HEADS = 8
PEER_NKEYS = 128
PEER_TOPK = 16
N_MOD = 6

LANES = 128
SUBLANES = 8
VMEM_LIMIT = 56 * 1024 * 1024

NEG_INF = float("-inf")


def _dot(a, b):
    return jnp.dot(a, b, preferred_element_type=F32)


def _dot_nt(a, b):
    return lax.dot_general(a, b, (((1,), (1,)), ((), ())), preferred_element_type=F32)


def _gelu(x):
    return 0.5 * x * (1.0 + lax.erf(x * (0.5 ** 0.5)))


def _params(*sem):
    return pltpu.CompilerParams(dimension_semantics=sem, vmem_limit_bytes=VMEM_LIMIT)


def _ada_kernel(c_ref, w_ref, b_ref, o_ref):
    o_ref[...] = jnp.dot(c_ref[...], w_ref[...], preferred_element_type=F32,
                         precision=lax.Precision.HIGHEST) + b_ref[...]


def _ada(c, w_ada, b_ada):
    depth = w_ada.shape[0]
    c8 = jnp.broadcast_to(c, (SUBLANES, D_MODEL))
    out = pl.pallas_call(
        _ada_kernel,
        grid=(depth, N_MOD),
        in_specs=[pl.BlockSpec((SUBLANES, D_MODEL), lambda l, j: (0, 0)),
                  pl.BlockSpec((None, D_MODEL, D_MODEL), lambda l, j: (l, 0, j)),
                  pl.BlockSpec((None, 1, D_MODEL), lambda l, j: (l, 0, j))],
        out_specs=pl.BlockSpec((None, SUBLANES, D_MODEL), lambda l, j: (l, 0, j)),
        out_shape=jax.ShapeDtypeStruct((depth, SUBLANES, N_MOD * D_MODEL), F32),
        compiler_params=_params("parallel", "parallel"),
        name="ada_mod",
    )(c8, w_ada, b_ada.reshape(depth, 1, N_MOD * D_MODEL))
    return out[:, 0, :]


def _rope_kernel(pos_ref, freq_ref, sign_ref, cos_ref, sin_ref):
    ang = pos_ref[...].astype(F32) * freq_ref[...]
    cos_ref[...] = jnp.cos(ang)
    sin_ref[...] = jnp.sin(ang) * sign_ref[...]


def _rope_tables(positions, tm):
    s = positions.shape[-1]
    half = HEAD_DIM // 2
    freq = ROPE_THETA ** (-jnp.arange(half, dtype=F32) / half)
    freq = jnp.tile(freq, LANES // half).reshape(1, LANES)
    sign = jnp.tile(jnp.concatenate([-jnp.ones((half,), F32), jnp.ones((half,), F32)]), LANES // HEAD_DIM)
    sign = sign.reshape(1, LANES)
    row = pl.BlockSpec((1, LANES), lambda i: (0, 0))
    return pl.pallas_call(
        _rope_kernel,
        grid=(s // tm,),
        in_specs=[pl.BlockSpec((tm, 1), lambda i: (i, 0)), row, row],
        out_specs=[pl.BlockSpec((tm, LANES), lambda i: (i, 0))] * 2,
        out_shape=[jax.ShapeDtypeStruct((s, LANES), F32)] * 2,
        compiler_params=_params("parallel"),
        name="rope_tables",
    )(positions.reshape(s, 1), freq, sign)


def _rms_mod(x, g, sc, sh):
    ms = jnp.mean(x * x, axis=-1, keepdims=True)
    y = x * lax.rsqrt(ms + EPS) * g
    return y * (1.0 + sc) + sh


def _inproj_kernel(x_ref, g_ref, sc_ref, sh_ref, w_ref, qg_ref, kg_ref, cos_ref, sin_ref, bd_ref, cw_ref,
                   q_ref, k_ref, v_ref, conv_ref, ubuf_ref, *, tm):
    i = pl.program_id(0)
    h = _rms_mod(x_ref[...], g_ref[...], sc_ref[...], sh_ref[...])
    proj = _dot(h.astype(BF16), w_ref[...])
    cos = cos_ref[...]
    sin = sin_ref[...]
    lane = lax.broadcasted_iota(jnp.int32, (tm, LANES), 1)
    upper = (lane & (HEAD_DIM // 2)) != 0

    def head_norm_rope(z, gain_ref, out_ref, scale):
        for j in range(ATTN_WIDTH // 256):
            zj = z[:, 256 * j:256 * (j + 1)]
            ss = _dot((zj * zj).astype(BF16), bd_ref[...])
            zn = zj * lax.rsqrt(ss * (1.0 / HEAD_DIM) + EPS) * gain_ref[:, 256 * j:256 * (j + 1)]
            for t in range(2):
                zt = zn[:, LANES * t:LANES * (t + 1)]
                partner = jnp.where(upper, pltpu.roll(zt, HEAD_DIM // 2, 1),
                                    pltpu.roll(zt, LANES - HEAD_DIM // 2, 1))
                r = zt * cos + partner * sin
                c0 = 256 * j + LANES * t
                out_ref[:, c0:c0 + LANES] = (r * scale).astype(BF16)

    a = ATTN_WIDTH
    head_norm_rope(proj[:, 0:a], qg_ref, q_ref, HEAD_DIM ** -0.5)
    head_norm_rope(proj[:, a:2 * a], kg_ref, k_ref, 1.0)
    v_ref[...] = proj[:, 2 * a:3 * a].astype(BF16)

    c = CONV_WIDTH
    bg = proj[:, 3 * a:3 * a + c]
    cg = proj[:, 3 * a + c:3 * a + 2 * c]
    xv = proj[:, 3 * a + 2 * c:3 * a + 3 * c]
    u = cg * xv

    @pl.when(i == 0)
    def _():
        ubuf_ref[0:SUBLANES, :] = jnp.zeros((SUBLANES, c), F32)

    ubuf_ref[SUBLANES:SUBLANES + tm, :] = u
    u1 = ubuf_ref[SUBLANES - 1:SUBLANES - 1 + tm, :]
    u2 = ubuf_ref[SUBLANES - 2:SUBLANES - 2 + tm, :]
    y = u2 * cw_ref[0:1, :] + u1 * cw_ref[1:2, :] + u * cw_ref[2:3, :]
    conv_ref[...] = (bg * y).astype(BF16)
    ubuf_ref[0:SUBLANES, :] = ubuf_ref[tm:tm + SUBLANES, :]


def _inproj(x, g, sc, sh, w_in, qg, kg, cos_t, sin_t, bd, conv_w, tm):
    s = x.shape[0]
    row = lambda n: pl.BlockSpec((1, n), lambda i: (0, 0))
    tok = lambda n: pl.BlockSpec((tm, n), lambda i: (i, 0))
    return pl.pallas_call(
        functools.partial(_inproj_kernel, tm=tm),
        grid=(s // tm,),
        in_specs=[tok(D_MODEL), row(D_MODEL), row(D_MODEL), row(D_MODEL),
                  pl.BlockSpec((D_MODEL, IN_PROJ_WIDTH), lambda i: (0, 0)),
                  row(ATTN_WIDTH), row(ATTN_WIDTH), tok(LANES), tok(LANES),
                  pl.BlockSpec((256, 256), lambda i: (0, 0)),
                  pl.BlockSpec((3, CONV_WIDTH), lambda i: (0, 0))],
        out_specs=[tok(ATTN_WIDTH), tok(ATTN_WIDTH), tok(ATTN_WIDTH), tok(CONV_WIDTH)],
        out_shape=[jax.ShapeDtypeStruct((s, ATTN_WIDTH), BF16)] * 3
        + [jax.ShapeDtypeStruct((s, CONV_WIDTH), BF16)],
        scratch_shapes=[pltpu.VMEM((tm + SUBLANES, CONV_WIDTH), F32)],
        compiler_params=_params("arbitrary"),
        name="inproj",
    )(x, g, sc, sh, w_in, qg, kg, cos_t, sin_t, bd, conv_w)


def _attn_kernel(*refs, first, last):
    tq = BRANCH_KEYS
    if first:
        q_ref, kp_ref, kc_ref, vp_ref, vc_ref, acc_out, ml_out = refs
    elif last:
        q_ref, kp_ref, kc_ref, vp_ref, vc_ref, acc_in, ml_in, out_ref = refs
    else:
        q_ref, kp_ref, kc_ref, vp_ref, vc_ref, acc_in, ml_in, acc_out, ml_out = refs
    ib = pl.program_id(1)
    qi = lax.broadcasted_iota(jnp.int32, (tq, tq), 0)
    ki = lax.broadcasted_iota(jnp.int32, (tq, tq), 1)
    mask_c = ki <= qi
    mask_p = jnp.logical_and(ki >= qi, ib > 0)
    lane = ki
    ml_old = None if first else ml_in[...]
    ml_new = jnp.zeros((tq, LANES), F32)
    for h in range(N_ATTN_HEADS):
        sl = slice(HEAD_DIM * h, HEAD_DIM * (h + 1))
        qh = q_ref[:, sl]
        sp = jnp.where(mask_p, _dot_nt(qh, kp_ref[:, sl]), NEG_INF)
        sc = jnp.where(mask_c, _dot_nt(qh, kc_ref[:, sl]), NEG_INF)
        m_cur = jnp.maximum(jnp.max(sp, axis=1, keepdims=True), jnp.max(sc, axis=1, keepdims=True))
        if first:
            m_new = m_cur
        else:
            m_old = jnp.max(jnp.where(lane == h, ml_old, NEG_INF), axis=1, keepdims=True)
            l_old = jnp.sum(jnp.where(lane == 16 + h, ml_old, 0.0), axis=1, keepdims=True)
            m_new = jnp.maximum(m_old, m_cur)
            alpha = jnp.exp(m_old - m_new)
        pp = jnp.exp(sp - m_new)
        pc = jnp.exp(sc - m_new)
        l_cur = jnp.sum(pp, axis=1, keepdims=True) + jnp.sum(pc, axis=1, keepdims=True)
        pv = _dot(pp.astype(BF16), vp_ref[:, sl]) + _dot(pc.astype(BF16), vc_ref[:, sl])
        if first:
            l_new, acc = l_cur, pv
        else:
            l_new = alpha * l_old + l_cur
            acc = alpha * acc_in[:, sl] + pv
        if last:
            out_ref[:, sl] = (acc / l_new).astype(BF16)
        else:
            acc_out[:, sl] = acc
            ml_new = jnp.where(lane == h, m_new, ml_new)
            ml_new = jnp.where(lane == 16 + h, l_new, ml_new)
    if not last:
        ml_out[...] = ml_new


def _attn_pass(q, k, v, state, dil, first, last):
    s = q.shape[0]
    n = s // dil
    tq = BRANCH_KEYS
    view = lambda a: a.reshape(n, dil * a.shape[1])
    cur = lambda w: pl.BlockSpec((tq, w), lambda r, ib: (ib, r))
    prev = lambda w: pl.BlockSpec((tq, w), lambda r, ib: (jnp.maximum(ib - 1, 0), r))
    aw = ATTN_WIDTH
    in_specs = [cur(aw), prev(aw), cur(aw), prev(aw), cur(aw)]
    args = [view(q), view(k), view(k), view(v), view(v)]
    if not first:
        in_specs += [cur(aw), cur(LANES)]
        args += [view(state[0]), view(state[1])]
    if last:
        out_specs = [cur(aw)]
        out_shape = [jax.ShapeDtypeStruct((n, dil * aw), BF16)]
    else:
        out_specs = [cur(aw), cur(LANES)]
        out_shape = [jax.ShapeDtypeStruct((n, dil * aw), F32), jax.ShapeDtypeStruct((n, dil * LANES), F32)]
    outs = pl.pallas_call(
        functools.partial(_attn_kernel, first=first, last=last),
        grid=(dil, n // tq),
        in_specs=in_specs, out_specs=out_specs, out_shape=out_shape,
        compiler_params=_params("parallel", "parallel"),
        name=f"attn_d{dil}",
    )(*args)
    return [o.reshape(s, -1) for o in outs]


def _attention(q, k, v):
    state = None
    for idx, dil in enumerate(DILATIONS):
        state = _attn_pass(q, k, v, state, dil, first=idx == 0, last=idx == len(DILATIONS) - 1)
    return state[0]


def _outproj_kernel(x_ref, attn_ref, conv_ref, woa_ref, woc_ref, gt_ref, g_ref, sc_ref, sh_ref, wq_ref,
                    keys_ref, x1_ref, h2_ref, st_ref):
    mix = _dot(attn_ref[...], woa_ref[...]) + _dot(conv_ref[...], woc_ref[...])
    x1 = x_ref[...] + gt_ref[...] * mix
    x1_ref[...] = x1
    h2 = _rms_mod(x1, g_ref[...], sc_ref[...], sh_ref[...]).astype(BF16)
    h2_ref[...] = h2
    qp = _dot(h2, wq_ref[...])
    for g in range(2 * PEER_HEADS):
        qg = qp[:, PEER_NKEYS * g:PEER_NKEYS * (g + 1)].astype(BF16)
        st_ref[PEER_NKEYS * g:PEER_NKEYS * (g + 1), :] = _dot_nt(keys_ref[g], qg)


def _outproj(x, attn, conv, wo_a, wo_c, gt, g, sc, sh, wq, keys, tm):
    s = x.shape[0]
    nq = wq.shape[1]
    row = lambda n: pl.BlockSpec((1, n), lambda i: (0, 0))
    tok = lambda n: pl.BlockSpec((tm, n), lambda i: (i, 0))
    full = lambda a: pl.BlockSpec(a.shape, lambda i: (0,) * a.ndim)
    return pl.pallas_call(
        _outproj_kernel,
        grid=(s // tm,),
        in_specs=[tok(D_MODEL), tok(ATTN_WIDTH), tok(CONV_WIDTH), full(wo_a), full(wo_c),
                  row(D_MODEL), row(D_MODEL), row(D_MODEL), row(D_MODEL), full(wq), full(keys)],
        out_specs=[tok(D_MODEL), tok(D_MODEL), pl.BlockSpec((nq, tm), lambda i: (0, i))],
        out_shape=[jax.ShapeDtypeStruct((s, D_MODEL), F32), jax.ShapeDtypeStruct((s, D_MODEL), BF16),
                   jax.ShapeDtypeStruct((nq, s), F32)],
        compiler_params=_params("parallel"),
        name="outproj",
    )(x, attn, conv, wo_a, wo_c, gt, g, sc, sh, wq, keys)


def _top16(s, tb):
    n = s.shape[0]
    row = lax.broadcasted_iota(jnp.int32, (n, tb), 0)
    row16 = lax.broadcasted_iota(jnp.int32, (PEER_TOPK, tb), 0)
    cur = s
    rank = jnp.full((n, tb), PEER_TOPK, jnp.int32)
    vals = jnp.zeros((PEER_TOPK, tb), F32)
    for kk in range(PEER_TOPK):
        m = jnp.max(cur, axis=0, keepdims=True)
        idx = jnp.min(jnp.where(cur == m, row, n), axis=0, keepdims=True)
        sel = row == idx
        rank = jnp.where(sel, kk, rank)
        cur = jnp.where(sel, NEG_INF, cur)
        vals = jnp.where(row16 == kk, m, vals)
    return vals, rank


def _topk_kernel(st_ref, rank2_ref, cnt_ref, e1_ref, e2_ref, *, tb):
    kk = PEER_TOPK
    s1 = st_ref[0:PEER_NKEYS, :]
    s2 = st_ref[PEER_NKEYS:2 * PEER_NKEYS, :]
    v1, rank1 = _top16(s1, tb)
    v2, rank2 = _top16(s2, tb)

    sub = lax.broadcasted_iota(jnp.int32, (SUBLANES, tb), 0)
    v2lo, v2hi = v2[0:SUBLANES, :], v2[SUBLANES:kk, :]
    groups = [(v1[0:1, :] + v2lo, sub, 0), (v1[0:1, :] + v2hi, sub + SUBLANES, 0),
              (v1[1:2, :] + v2lo, sub + kk, 1)]
    for a in range(2, SUBLANES):
        nb = kk // (a + 1)
        groups.append((jnp.where(sub < nb, v1[a:a + 1, :] + v2lo, NEG_INF), sub + kk * a, a))
    groups.append((v1[SUBLANES:kk, :] + v2[0:1, :], (sub + SUBLANES) * kk, None))
    cand = [g[0] for g in groups]
    flat = [g[1] for g in groups]
    cur = list(cand)
    big = kk * kk
    for _ in range(kk):
        m = functools.reduce(jnp.maximum, cur)
        m = jnp.max(m, axis=0, keepdims=True)
        hit = [jnp.where(c == m, f, big) for c, f in zip(cur, flat)]
        idx = jnp.min(functools.reduce(jnp.minimum, hit), axis=0, keepdims=True)
        cur = [jnp.where(f == idx, NEG_INF, c) for c, f in zip(cur, flat)]
    sel = [jnp.logical_and(c == NEG_INF, o != NEG_INF) for c, o in zip(cur, cand)]

    one = lambda mk: jnp.where(mk, 1.0, 0.0)
    counts = [jnp.sum(one(sel[0]) + one(sel[1]), axis=0, keepdims=True)]
    for gi in range(2, 2 + SUBLANES - 1):
        counts.append(jnp.sum(one(sel[gi]), axis=0, keepdims=True))
    tail = one(sel[-1])
    cnt = jnp.zeros((PEER_NKEYS, tb), F32)
    for a in range(kk):
        na = counts[a] if a < SUBLANES else tail[a - SUBLANES:a - SUBLANES + 1, :]
        cnt = jnp.where(rank1 == a, na, cnt)

    m1, m2 = v1[0:1, :], v2[0:1, :]
    mt = m1 + m2
    z = functools.reduce(lambda x, y: x + y,
                         [jnp.sum(jnp.where(sl, jnp.exp(o - mt), 0.0), axis=0, keepdims=True)
                          for sl, o in zip(sel, cand)])
    rank2_ref[...] = pltpu.bitcast(rank2.astype(F32).astype(BF16), jnp.uint32)
    cnt_ref[...] = cnt
    e1_ref[...] = jnp.exp(s1 - m1)
    e2_ref[...] = pltpu.bitcast((jnp.exp(s2 - m2) / z).astype(BF16), jnp.uint32)


def _topk(st, tb):
    s = st.shape[1]
    crow = PEER_NKEYS * SUBLANES // PACK
    plane = pl.BlockSpec((None, PEER_NKEYS, tb), lambda h, i: (h, 0, i))
    cplane = pl.BlockSpec((None, crow, tb), lambda h, i: (h, 0, i))
    shape = jax.ShapeDtypeStruct((PEER_HEADS, PEER_NKEYS, s), F32)
    cshape = jax.ShapeDtypeStruct((PEER_HEADS, crow, s), jnp.uint32)
    return pl.pallas_call(
        functools.partial(_topk_kernel, tb=tb),
        grid=(PEER_HEADS, s // tb),
        in_specs=[pl.BlockSpec((2 * PEER_NKEYS, tb), lambda h, i: (h, i))],
        out_specs=[cplane, plane, plane, cplane],
        out_shape=[cshape, shape, shape, cshape],
        compiler_params=_params("parallel", "parallel"),
        name="peer_topk",
    )(st)


PACK = 16


def _peer_kernel(h_ref, u_ref, vt_ref, rank2_ref, cnt_ref, e1_ref, e2_ref, x1_ref, gt_ref,
                 out_ref, acc_ref, p_ref, *, tm, ec):
    j = pl.program_id(1)
    rows = ec // PEER_NKEYS

    @pl.when(j == 0)
    def _():
        acc_ref[...] = jnp.zeros_like(acc_ref)

    crows = pl.ds(pl.multiple_of(j * rows, SUBLANES), rows)
    nt = tm // LANES
    er = ec // nt
    erw = er * SUBLANES // PACK
    zt = []
    for t in range(nt):
        u_t = pltpu.bitcast(u_ref[erw * t:erw * (t + 1), :], BF16)
        zt.append(_dot_nt(u_t, h_ref[...]))
        tl = slice(LANES * t, LANES * (t + 1))
        cnt_rows = [cnt_ref[hh, crows, tl] for hh in range(PEER_HEADS)]
        e1_rows = [e1_ref[hh, crows, tl] for hh in range(PEER_HEADS)]
        for r in range(rows):
            w = [None] * (PEER_NKEYS // PACK)
            for hh in range(PEER_HEADS):
                cb = jnp.broadcast_to(cnt_rows[hh][r:r + 1, :], (PACK, LANES)).astype(BF16)
                eb = jnp.broadcast_to(e1_rows[hh][r:r + 1, :], (PACK, LANES)).astype(BF16)
                for g in range(PEER_NKEYS // PACK):
                    gs = slice(SUBLANES * g, SUBLANES * (g + 1))
                    rk = pltpu.bitcast(rank2_ref[hh, gs, tl], BF16)
                    e2 = pltpu.bitcast(e2_ref[hh, gs, tl], BF16)
                    term = jnp.where(rk < cb, e2, 0.0) * eb
                    w[g] = term if w[g] is None else w[g] + term
            p_ref[PEER_NKEYS * r:PEER_NKEYS * (r + 1), tl] = jnp.concatenate(w, axis=0)
    for t in range(nt):
        rs = slice(er * t, er * (t + 1))
        p_ref[rs, :] = p_ref[rs, :] * _gelu(zt[t].astype(BF16))
    acc_ref[...] += _dot(pltpu.bitcast(vt_ref[...], BF16), p_ref[...])

    @pl.when(j == pl.num_programs(1) - 1)
    def _():
        out_ref[...] = x1_ref[...] + gt_ref[...] * acc_ref[...].T


def _peer(h2, u, vt, planes, x1, gt, tm, ec):
    s = h2.shape[0]
    ne = vt.shape[1]
    assert ec == SUBLANES * PEER_NKEYS, ec
    plane = pl.BlockSpec((PEER_HEADS, PEER_NKEYS, tm), lambda i, j: (0, 0, i))
    cplane = pl.BlockSpec((PEER_HEADS, PEER_NKEYS * SUBLANES // PACK, tm), lambda i, j: (0, 0, i))
    tok = pl.BlockSpec((tm, D_MODEL), lambda i, j: (i, 0))
    return pl.pallas_call(
        functools.partial(_peer_kernel, tm=tm, ec=ec),
        grid=(s // tm, ne // ec),
        in_specs=[tok, pl.BlockSpec((ec * SUBLANES // PACK, D_MODEL), lambda i, j: (j, 0)),
                  pl.BlockSpec((D_MODEL * SUBLANES // PACK, ec), lambda i, j: (0, j)),
                  cplane, plane, plane, cplane, tok, pl.BlockSpec((1, D_MODEL), lambda i, j: (0, 0))],
        out_specs=tok,
        out_shape=jax.ShapeDtypeStruct((s, D_MODEL), F32),
        scratch_shapes=[pltpu.VMEM((D_MODEL, tm), F32), pltpu.VMEM((ec, tm), BF16)],
        compiler_params=_params("parallel", "arbitrary"),
        name="peer_dense",
    )(h2, u, vt, *planes, x1, gt)


def _to_words(x):
    if x.dtype.itemsize == 4:
        return lax.bitcast_convert_type(x, jnp.uint32)
    m, n = x.shape
    return lax.bitcast_convert_type(jnp.swapaxes(x.reshape(m // 2, 2, n), -1, -2), jnp.uint32)


def _tile(s, want):
    t = min(want, s)
    assert s % t == 0, (s, t)
    return t


def _layer(x, mod, cos_t, sin_t, bd, norm_mix, norm_ffn, w_in, q_norm, k_norm, conv_w, w_out,
           peer_wq, peer_keys, peer_u, peer_v):
    s = x.shape[0]
    sh1, sc1, gt1, sh2, sc2, gt2 = [m.reshape(1, D_MODEL) for m in jnp.split(mod, N_MOD)]
    tm = _tile(s, 512)
    qg = jnp.tile(q_norm, N_ATTN_HEADS).reshape(1, ATTN_WIDTH)
    kg = jnp.tile(k_norm, N_ATTN_HEADS).reshape(1, ATTN_WIDTH)
    q, k, v, conv = _inproj(x, norm_mix.reshape(1, D_MODEL), sc1, sh1, w_in.astype(BF16), qg, kg,
                            cos_t, sin_t, bd, conv_w, tm)
    attn = _attention(q, k, v)
    wo = w_out.astype(BF16)
    keys = peer_keys.reshape(2 * PEER_HEADS, PEER_NKEYS, -1).astype(BF16)
    x1, h2, st = _outproj(x, attn, conv, wo[:ATTN_WIDTH], wo[ATTN_WIDTH:], gt1,
                          norm_ffn.reshape(1, D_MODEL), sc2, sh2, peer_wq.astype(BF16), keys, tm)
    planes = _topk(st, _tile(s, 256))
    return _peer(h2, _to_words(peer_u.astype(BF16)), _to_words(peer_v.T.astype(BF16)), planes, x1, gt2,
                 _tile(s, 512), SUBLANES * PEER_NKEYS)


def kernel(x, c, positions, w_ada, b_ada, norm_mix, norm_ffn, w_in, q_norm, k_norm, conv_w, w_out,
           peer_wq, peer_keys, peer_u, peer_v):
    b, s, d = x.shape
    assert b == 1 and d == D_MODEL and s % (max(DILATIONS) * BRANCH_KEYS) == 0, x.shape
    depth = w_ada.shape[0]
    mod = _ada(c, w_ada, b_ada)
    cos_t, sin_t = _rope_tables(positions, _tile(s, 1024))
    blk = jnp.arange(256) // HEAD_DIM
    bd = (blk[:, None] == blk[None, :]).astype(BF16)
    y = x.reshape(s, d)
    for l in range(depth):
        y = _layer(y, mod[l], cos_t, sin_t, bd, norm_mix[l], norm_ffn[l], w_in[l], q_norm[l], k_norm[l],
                   conv_w[l], w_out[l], peer_wq[l], peer_keys[l], peer_u[l], peer_v[l])
    return y.reshape(b, s, d)
```

```python
import functools

import jax
import jax.numpy as jnp
from jax import lax
from jax.experimental import pallas as pl
from jax.experimental.pallas import tpu as pltpu

F32 = jnp.float32
BF16 = jnp.bfloat16

D_MODEL = 1024
HEAD_DIM = 64
N_ATTN_HEADS = 12
ATTN_WIDTH = N_ATTN_HEADS * HEAD_DIM
CONV_WIDTH = D_MODEL - ATTN_WIDTH
IN_PROJ_WIDTH = 3 * ATTN_WIDTH + 3 * CONV_WIDTH
BRANCH_KEYS = 128
DILATIONS = (16, 4, 1)
ROPE_THETA = 10000.0
EPS = 1e-6
PEER_HEADS = 8
PEER_NKEYS = 128
PEER_TOPK = 16
N_MOD = 6

LANES = 128
SUBLANES = 8
VMEM_LIMIT = 56 * 1024 * 1024

NEG_INF = float("-inf")


def _dot(a, b):
    return jnp.dot(a, b, preferred_element_type=F32)


def _dot_nt(a, b):
    return lax.dot_general(a, b, (((1,), (1,)), ((), ())), preferred_element_type=F32)


def _gelu(x):
    return 0.5 * x * (1.0 + lax.erf(x * (0.5 ** 0.5)))


def _params(*sem):
    return pltpu.CompilerParams(dimension_semantics=sem, vmem_limit_bytes=VMEM_LIMIT)


def _ada_kernel(c_ref, w_ref, b_ref, o_ref):
    o_ref[...] = jnp.dot(c_ref[...], w_ref[...], preferred_element_type=F32,
                         precision=lax.Precision.HIGHEST) + b_ref[...]


def _ada(c, w_ada, b_ada):
    depth = w_ada.shape[0]
    c8 = jnp.broadcast_to(c, (SUBLANES, D_MODEL))
    out = pl.pallas_call(
        _ada_kernel,
        grid=(depth, N_MOD),
        in_specs=[pl.BlockSpec((SUBLANES, D_MODEL), lambda l, j: (0, 0)),
                  pl.BlockSpec((None, D_MODEL, D_MODEL), lambda l, j: (l, 0, j)),
                  pl.BlockSpec((None, 1, D_MODEL), lambda l, j: (l, 0, j))],
        out_specs=pl.BlockSpec((None, SUBLANES, D_MODEL), lambda l, j: (l, 0, j)),
        out_shape=jax.ShapeDtypeStruct((depth, SUBLANES, N_MOD * D_MODEL), F32),
        compiler_params=_params("parallel", "parallel"),
        name="ada_mod",
    )(c8, w_ada, b_ada.reshape(depth, 1, N_MOD * D_MODEL))
    return out[:, 0, :]


def _rope_kernel(pos_ref, freq_ref, sign_ref, cos_ref, sin_ref):
    ang = pos_ref[...].astype(F32) * freq_ref[...]
    cos_ref[...] = jnp.cos(ang)
    sin_ref[...] = jnp.sin(ang) * sign_ref[...]


def _rope_tables(positions, tm):
    s = positions.shape[-1]
    half = HEAD_DIM // 2
    freq = ROPE_THETA ** (-jnp.arange(half, dtype=F32) / half)
    freq = jnp.tile(freq, LANES // half).reshape(1, LANES)
    sign = jnp.tile(jnp.concatenate([-jnp.ones((half,), F32), jnp.ones((half,), F32)]), LANES // HEAD_DIM)
    sign = sign.reshape(1, LANES)
    row = pl.BlockSpec((1, LANES), lambda i: (0, 0))
    return pl.pallas_call(
        _rope_kernel,
        grid=(s // tm,),
        in_specs=[pl.BlockSpec((tm, 1), lambda i: (i, 0)), row, row],
        out_specs=[pl.BlockSpec((tm, LANES), lambda i: (i, 0))] * 2,
        out_shape=[jax.ShapeDtypeStruct((s, LANES), F32)] * 2,
        compiler_params=_params("parallel"),
        name="rope_tables",
    )(positions.reshape(s, 1), freq, sign)


def _rms_mod(x, g, sc, sh):
    ms = jnp.mean(x * x, axis=-1, keepdims=True)
    y = x * lax.rsqrt(ms + EPS) * g
    return y * (1.0 + sc) + sh


def _inproj_kernel(x_ref, g_ref, sc_ref, sh_ref, w_ref, qg_ref, kg_ref, cos_ref, sin_ref, bd_ref, cw_ref,
                   q_ref, k_ref, v_ref, conv_ref, ubuf_ref, *, tm):
    i = pl.program_id(0)
    h = _rms_mod(x_ref[...], g_ref[...], sc_ref[...], sh_ref[...])
    proj = _dot(h.astype(BF16), w_ref[...])
    cos = cos_ref[...]
    sin = sin_ref[...]
    lane = lax.broadcasted_iota(jnp.int32, (tm, LANES), 1)
    upper = (lane & (HEAD_DIM // 2)) != 0

    def head_norm_rope(z, gain_ref, out_ref, scale):
        for j in range(ATTN_WIDTH // 256):
            zj = z[:, 256 * j:256 * (j + 1)]
            ss = _dot((zj * zj).astype(BF16), bd_ref[...])
            zn = zj * lax.rsqrt(ss * (1.0 / HEAD_DIM) + EPS) * gain_ref[:, 256 * j:256 * (j + 1)]
            for t in range(2):
                zt = zn[:, LANES * t:LANES * (t + 1)]
                partner = jnp.where(upper, pltpu.roll(zt, HEAD_DIM // 2, 1),
                                    pltpu.roll(zt, LANES - HEAD_DIM // 2, 1))
                r = zt * cos + partner * sin
                c0 = 256 * j + LANES * t
                out_ref[:, c0:c0 + LANES] = (r * scale).astype(BF16)

    a = ATTN_WIDTH
    head_norm_rope(proj[:, 0:a], qg_ref, q_ref, HEAD_DIM ** -0.5)
    head_norm_rope(proj[:, a:2 * a], kg_ref, k_ref, 1.0)
    v_ref[...] = proj[:, 2 * a:3 * a].astype(BF16)

    c = CONV_WIDTH
    bg = proj[:, 3 * a:3 * a + c]
    cg = proj[:, 3 * a + c:3 * a + 2 * c]
    xv = proj[:, 3 * a + 2 * c:3 * a + 3 * c]
    u = cg * xv

    @pl.when(i == 0)
    def _():
        ubuf_ref[0:SUBLANES, :] = jnp.zeros((SUBLANES, c), F32)

    ubuf_ref[SUBLANES:SUBLANES + tm, :] = u
    u1 = ubuf_ref[SUBLANES - 1:SUBLANES - 1 + tm, :]
    u2 = ubuf_ref[SUBLANES - 2:SUBLANES - 2 + tm, :]
    y = u2 * cw_ref[0:1, :] + u1 * cw_ref[1:2, :] + u * cw_ref[2:3, :]
    conv_ref[...] = (bg * y).astype(BF16)
    ubuf_ref[0:SUBLANES, :] = ubuf_ref[tm:tm + SUBLANES, :]


def _inproj(x, g, sc, sh, w_in, qg, kg, cos_t, sin_t, bd, conv_w, tm):
    s = x.shape[0]
    row = lambda n: pl.BlockSpec((1, n), lambda i: (0, 0))
    tok = lambda n: pl.BlockSpec((tm, n), lambda i: (i, 0))
    return pl.pallas_call(
        functools.partial(_inproj_kernel, tm=tm),
        grid=(s // tm,),
        in_specs=[tok(D_MODEL), row(D_MODEL), row(D_MODEL), row(D_MODEL),
                  pl.BlockSpec((D_MODEL, IN_PROJ_WIDTH), lambda i: (0, 0)),
                  row(ATTN_WIDTH), row(ATTN_WIDTH), tok(LANES), tok(LANES),
                  pl.BlockSpec((256, 256), lambda i: (0, 0)),
                  pl.BlockSpec((3, CONV_WIDTH), lambda i: (0, 0))],
        out_specs=[tok(ATTN_WIDTH), tok(ATTN_WIDTH), tok(ATTN_WIDTH), tok(CONV_WIDTH)],
        out_shape=[jax.ShapeDtypeStruct((s, ATTN_WIDTH), BF16)] * 3
        + [jax.ShapeDtypeStruct((s, CONV_WIDTH), BF16)],
        scratch_shapes=[pltpu.VMEM((tm + SUBLANES, CONV_WIDTH), F32)],
        compiler_params=_params("arbitrary"),
        name="inproj",
    )(x, g, sc, sh, w_in, qg, kg, cos_t, sin_t, bd, conv_w)


def _attn_kernel(*refs, first, last):
    tq = BRANCH_KEYS
    if first:
        q_ref, kp_ref, kc_ref, vp_ref, vc_ref, acc_out, ml_out = refs
    elif last:
        q_ref, kp_ref, kc_ref, vp_ref, vc_ref, acc_in, ml_in, out_ref = refs
    else:
        q_ref, kp_ref, kc_ref, vp_ref, vc_ref, acc_in, ml_in, acc_out, ml_out = refs
    ib = pl.program_id(1)
    qi = lax.broadcasted_iota(jnp.int32, (tq, tq), 0)
    ki = lax.broadcasted_iota(jnp.int32, (tq, tq), 1)
    mask_c = ki <= qi
    mask_p = jnp.logical_and(ki >= qi, ib > 0)
    lane = ki
    ml_old = None if first else ml_in[...]
    heads = range(N_ATTN_HEADS)
    sls = [slice(HEAD_DIM * h, HEAD_DIM * (h + 1)) for h in heads]
    sp = [jnp.where(mask_p, _dot_nt(q_ref[:, sl], kp_ref[:, sl]), NEG_INF) for sl in sls]
    sc = [jnp.where(mask_c, _dot_nt(q_ref[:, sl], kc_ref[:, sl]), NEG_INF) for sl in sls]
    m_new = [jnp.max(jnp.maximum(a, b), axis=1, keepdims=True) for a, b in zip(sp, sc)]
    if not first:
        m_old = [jnp.max(jnp.where(lane == h, ml_old, NEG_INF), axis=1, keepdims=True) for h in heads]
        l_old = [jnp.sum(jnp.where(lane == 16 + h, ml_old, 0.0), axis=1, keepdims=True) for h in heads]
        m_new = [jnp.maximum(a, b) for a, b in zip(m_old, m_new)]
        alpha = [jnp.exp(a - b) for a, b in zip(m_old, m_new)]
    pp = [jnp.exp(s - m) for s, m in zip(sp, m_new)]
    pc = [jnp.exp(s - m) for s, m in zip(sc, m_new)]
    l_new = [jnp.sum(a + b, axis=1, keepdims=True) for a, b in zip(pp, pc)]
    acc = [_dot(a.astype(BF16), vp_ref[:, sl]) + _dot(b.astype(BF16), vc_ref[:, sl])
           for a, b, sl in zip(pp, pc, sls)]
    if not first:
        l_new = [a * lo + ln for a, lo, ln in zip(alpha, l_old, l_new)]
        acc = [a * acc_in[:, sl] + pv for a, sl, pv in zip(alpha, sls, acc)]
    if last:
        for sl, a, l in zip(sls, acc, l_new):
            out_ref[:, sl] = (a / l).astype(BF16)
    else:
        ml_new = jnp.zeros((tq, LANES), F32)
        for h, sl in zip(heads, sls):
            acc_out[:, sl] = acc[h]
            ml_new = jnp.where(lane == h, m_new[h], ml_new)
            ml_new = jnp.where(lane == 16 + h, l_new[h], ml_new)
        ml_out[...] = ml_new


def _attn_pass(q, k, v, state, dil, first, last):
    s = q.shape[0]
    n = s // dil
    tq = BRANCH_KEYS
    view = lambda a: a.reshape(n, dil * a.shape[1])
    cur = lambda w: pl.BlockSpec((tq, w), lambda r, ib: (ib, r))
    prev = lambda w: pl.BlockSpec((tq, w), lambda r, ib: (jnp.maximum(ib - 1, 0), r))
    aw = ATTN_WIDTH
    in_specs = [cur(aw), prev(aw), cur(aw), prev(aw), cur(aw)]
    args = [view(q), view(k), view(k), view(v), view(v)]
    if not first:
        in_specs += [cur(aw), cur(LANES)]
        args += [view(state[0]), view(state[1])]
    if last:
        out_specs = [cur(aw)]
        out_shape = [jax.ShapeDtypeStruct((n, dil * aw), BF16)]
    else:
        out_specs = [cur(aw), cur(LANES)]
        out_shape = [jax.ShapeDtypeStruct((n, dil * aw), F32), jax.ShapeDtypeStruct((n, dil * LANES), F32)]
    outs = pl.pallas_call(
        functools.partial(_attn_kernel, first=first, last=last),
        grid=(dil, n // tq),
        in_specs=in_specs, out_specs=out_specs, out_shape=out_shape,
        compiler_params=_params("parallel", "parallel"),
        name=f"attn_d{dil}",
    )(*args)
    return [o.reshape(s, -1) for o in outs]


def _attention(q, k, v):
    state = None
    for idx, dil in enumerate(DILATIONS):
        state = _attn_pass(q, k, v, state, dil, first=idx == 0, last=idx == len(DILATIONS) - 1)
    return state[0]


def _outproj_kernel(x_ref, attn_ref, conv_ref, woa_ref, woc_ref, gt_ref, g_ref, sc_ref, sh_ref, wq_ref,
                    keys_ref, x1_ref, h2_ref, st_ref):
    mix = _dot(attn_ref[...], woa_ref[...]) + _dot(conv_ref[...], woc_ref[...])
    x1 = x_ref[...] + gt_ref[...] * mix
    x1_ref[...] = x1
    h2 = _rms_mod(x1, g_ref[...], sc_ref[...], sh_ref[...]).astype(BF16)
    h2_ref[...] = h2
    qp = _dot(h2, wq_ref[...])
    for g in range(2 * PEER_HEADS):
        qg = qp[:, PEER_NKEYS * g:PEER_NKEYS * (g + 1)].astype(BF16)
        st_ref[PEER_NKEYS * g:PEER_NKEYS * (g + 1), :] = _dot_nt(keys_ref[g], qg)


def _outproj(x, attn, conv, wo_a, wo_c, gt, g, sc, sh, wq, keys, tm):
    s = x.shape[0]
    nq = wq.shape[1]
    row = lambda n: pl.BlockSpec((1, n), lambda i: (0, 0))
    tok = lambda n: pl.BlockSpec((tm, n), lambda i: (i, 0))
    full = lambda a: pl.BlockSpec(a.shape, lambda i: (0,) * a.ndim)
    return pl.pallas_call(
        _outproj_kernel,
        grid=(s // tm,),
        in_specs=[tok(D_MODEL), tok(ATTN_WIDTH), tok(CONV_WIDTH), full(wo_a), full(wo_c),
                  row(D_MODEL), row(D_MODEL), row(D_MODEL), row(D_MODEL), full(wq), full(keys)],
        out_specs=[tok(D_MODEL), tok(D_MODEL), pl.BlockSpec((nq, tm), lambda i: (0, i))],
        out_shape=[jax.ShapeDtypeStruct((s, D_MODEL), F32), jax.ShapeDtypeStruct((s, D_MODEL), BF16),
                   jax.ShapeDtypeStruct((nq, s), F32)],
        compiler_params=_params("parallel"),
        name="outproj",
    )(x, attn, conv, wo_a, wo_c, gt, g, sc, sh, wq, keys)


def _top16(ss, tb):
    n = ss[0].shape[0]
    row = lax.broadcasted_iota(jnp.int32, (n, tb), 0)
    row16 = lax.broadcasted_iota(jnp.int32, (PEER_TOPK, tb), 0)
    cur = list(ss)
    rank = [jnp.full((n, tb), PEER_TOPK, jnp.int32) for _ in ss]
    vals = [jnp.zeros((PEER_TOPK, tb), F32) for _ in ss]
    for kk in range(PEER_TOPK):
        m = [jnp.max(c, axis=0, keepdims=True) for c in cur]
        idx = [jnp.min(jnp.where(c == mm, row, n), axis=0, keepdims=True) for c, mm in zip(cur, m)]
        sel = [row == i for i in idx]
        rank = [jnp.where(sl, kk, r) for sl, r in zip(sel, rank)]
        cur = [jnp.where(sl, NEG_INF, c) for sl, c in zip(sel, cur)]
        vals = [jnp.where(row16 == kk, mm, v) for mm, v in zip(m, vals)]
    return vals, rank


def _topk_kernel(st_ref, rank2_ref, cnt_ref, e1_ref, e2_ref, *, tb):
    kk = PEER_TOPK
    s1 = st_ref[0:PEER_NKEYS, :]
    s2 = st_ref[PEER_NKEYS:2 * PEER_NKEYS, :]
    (v1, v2), (rank1, rank2) = _top16([s1, s2], tb)

    sub = lax.broadcasted_iota(jnp.int32, (SUBLANES, tb), 0)
    v2lo, v2hi = v2[0:SUBLANES, :], v2[SUBLANES:kk, :]
    groups = [(v1[0:1, :] + v2lo, sub, 0), (v1[0:1, :] + v2hi, sub + SUBLANES, 0),
              (v1[1:2, :] + v2lo, sub + kk, 1)]
    for a in range(2, SUBLANES):
        nb = kk // (a + 1)
        groups.append((jnp.where(sub < nb, v1[a:a + 1, :] + v2lo, NEG_INF), sub + kk * a, a))
    groups.append((v1[SUBLANES:kk, :] + v2[0:1, :], (sub + SUBLANES) * kk, None))
    cand = [g[0] for g in groups]
    flat = [g[1] for g in groups]
    cur = list(cand)
    big = kk * kk
    for _ in range(kk):
        m = functools.reduce(jnp.maximum, cur)
        m = jnp.max(m, axis=0, keepdims=True)
        hit = [jnp.where(c == m, f, big) for c, f in zip(cur, flat)]
        idx = jnp.min(functools.reduce(jnp.minimum, hit), axis=0, keepdims=True)
        cur = [jnp.where(f == idx, NEG_INF, c) for c, f in zip(cur, flat)]
    sel = [jnp.logical_and(c == NEG_INF, o != NEG_INF) for c, o in zip(cur, cand)]

    one = lambda mk: jnp.where(mk, 1.0, 0.0)
    counts = [jnp.sum(one(sel[0]) + one(sel[1]), axis=0, keepdims=True)]
    for gi in range(2, 2 + SUBLANES - 1):
        counts.append(jnp.sum(one(sel[gi]), axis=0, keepdims=True))
    tail = one(sel[-1])
    cnt = jnp.zeros((PEER_NKEYS, tb), F32)
    for a in range(kk):
        na = counts[a] if a < SUBLANES else tail[a - SUBLANES:a - SUBLANES + 1, :]
        cnt = jnp.where(rank1 == a, na, cnt)

    m1, m2 = v1[0:1, :], v2[0:1, :]
    mt = m1 + m2
    z = functools.reduce(lambda x, y: x + y,
                         [jnp.sum(jnp.where(sl, jnp.exp(o - mt), 0.0), axis=0, keepdims=True)
                          for sl, o in zip(sel, cand)])
    rank2_ref[...] = pltpu.bitcast(rank2.astype(F32).astype(BF16), jnp.uint32)
    cnt_ref[...] = cnt
    e1_ref[...] = jnp.exp(s1 - m1)
    e2_ref[...] = pltpu.bitcast((jnp.exp(s2 - m2) / z).astype(BF16), jnp.uint32)


def _topk(st, tb):
    s = st.shape[1]
    crow = PEER_NKEYS * SUBLANES // PACK
    plane = pl.BlockSpec((None, PEER_NKEYS, tb), lambda h, i: (h, 0, i))
    cplane = pl.BlockSpec((None, crow, tb), lambda h, i: (h, 0, i))
    shape = jax.ShapeDtypeStruct((PEER_HEADS, PEER_NKEYS, s), F32)
    cshape = jax.ShapeDtypeStruct((PEER_HEADS, crow, s), jnp.uint32)
    return pl.pallas_call(
        functools.partial(_topk_kernel, tb=tb),
        grid=(PEER_HEADS, s // tb),
        in_specs=[pl.BlockSpec((2 * PEER_NKEYS, tb), lambda h, i: (h, i))],
        out_specs=[cplane, plane, plane, cplane],
        out_shape=[cshape, shape, shape, cshape],
        compiler_params=_params("parallel", "parallel"),
        name="peer_topk",
    )(st)


PACK = 16


def _peer_kernel(h_ref, u_ref, vt_ref, rank2_ref, cnt_ref, e1_ref, e2_ref, x1_ref, gt_ref,
                 out_ref, acc_ref, p_ref, *, tm, ec):
    j = pl.program_id(1)
    rows = ec // PEER_NKEYS

    @pl.when(j == 0)
    def _():
        acc_ref[...] = jnp.zeros_like(acc_ref)

    crows = pl.ds(pl.multiple_of(j * rows, SUBLANES), rows)
    nt = tm // LANES
    er = ec // nt
    erw = er * SUBLANES // PACK
    zt = []
    for t in range(nt):
        u_t = pltpu.bitcast(u_ref[erw * t:erw * (t + 1), :], BF16)
        zt.append(_dot_nt(u_t, h_ref[...]))
        tl = slice(LANES * t, LANES * (t + 1))
        cnt_rows = [cnt_ref[hh, crows, tl] for hh in range(PEER_HEADS)]
        e1_rows = [e1_ref[hh, crows, tl] for hh in range(PEER_HEADS)]
        for r in range(rows):
            w = [None] * (PEER_NKEYS // PACK)
            for hh in range(PEER_HEADS):
                cb = jnp.broadcast_to(cnt_rows[hh][r:r + 1, :], (PACK, LANES)).astype(BF16)
                eb = jnp.broadcast_to(e1_rows[hh][r:r + 1, :], (PACK, LANES)).astype(BF16)
                for g in range(PEER_NKEYS // PACK):
                    gs = slice(SUBLANES * g, SUBLANES * (g + 1))
                    rk = pltpu.bitcast(rank2_ref[hh, gs, tl], BF16)
                    e2 = pltpu.bitcast(e2_ref[hh, gs, tl], BF16)
                    term = jnp.where(rk < cb, e2, 0.0) * eb
                    w[g] = term if w[g] is None else w[g] + term
            p_ref[PEER_NKEYS * r:PEER_NKEYS * (r + 1), tl] = jnp.concatenate(w, axis=0)
    for t in range(nt):
        rs = slice(er * t, er * (t + 1))
        p_ref[rs, :] = p_ref[rs, :] * _gelu(zt[t].astype(BF16))
    acc_ref[...] += _dot(pltpu.bitcast(vt_ref[...], BF16), p_ref[...])

    @pl.when(j == pl.num_programs(1) - 1)
    def _():
        out_ref[...] = x1_ref[...] + gt_ref[...] * acc_ref[...].T


def _peer(h2, u, vt, planes, x1, gt, tm, ec):
    s = h2.shape[0]
    ne = vt.shape[1]
    assert ec == SUBLANES * PEER_NKEYS, ec
    plane = pl.BlockSpec((PEER_HEADS, PEER_NKEYS, tm), lambda i, j: (0, 0, i))
    cplane = pl.BlockSpec((PEER_HEADS, PEER_NKEYS * SUBLANES // PACK, tm), lambda i, j: (0, 0, i))
    tok = pl.BlockSpec((tm, D_MODEL), lambda i, j: (i, 0))
    return pl.pallas_call(
        functools.partial(_peer_kernel, tm=tm, ec=ec),
        grid=(s // tm, ne // ec),
        in_specs=[tok, pl.BlockSpec((ec * SUBLANES // PACK, D_MODEL), lambda i, j: (j, 0)),
                  pl.BlockSpec((D_MODEL * SUBLANES // PACK, ec), lambda i, j: (0, j)),
                  cplane, plane, plane, cplane, tok, pl.BlockSpec((1, D_MODEL), lambda i, j: (0, 0))],
        out_specs=tok,
        out_shape=jax.ShapeDtypeStruct((s, D_MODEL), F32),
        scratch_shapes=[pltpu.VMEM((D_MODEL, tm), F32), pltpu.VMEM((ec, tm), BF16)],
        compiler_params=_params("parallel", "arbitrary"),
        name="peer_dense",
    )(h2, u, vt, *planes, x1, gt)


def _to_words(x):
    if x.dtype.itemsize == 4:
        return lax.bitcast_convert_type(x, jnp.uint32)
    m, n = x.shape
    return lax.bitcast_convert_type(jnp.swapaxes(x.reshape(m // 2, 2, n), -1, -2), jnp.uint32)


def _tile(s, want):
    t = min(want, s)
    assert s % t == 0, (s, t)
    return t


def _layer(x, mod, cos_t, sin_t, bd, norm_mix, norm_ffn, w_in, q_norm, k_norm, conv_w, w_out,
           peer_wq, peer_keys, peer_u, peer_v):
    s = x.shape[0]
    sh1, sc1, gt1, sh2, sc2, gt2 = [m.reshape(1, D_MODEL) for m in jnp.split(mod, N_MOD)]
    tm = _tile(s, 512)
    qg = jnp.tile(q_norm, N_ATTN_HEADS).reshape(1, ATTN_WIDTH)
    kg = jnp.tile(k_norm, N_ATTN_HEADS).reshape(1, ATTN_WIDTH)
    q, k, v, conv = _inproj(x, norm_mix.reshape(1, D_MODEL), sc1, sh1, w_in.astype(BF16), qg, kg,
                            cos_t, sin_t, bd, conv_w, tm)
    attn = _attention(q, k, v)
    wo = w_out.astype(BF16)
    keys = peer_keys.reshape(2 * PEER_HEADS, PEER_NKEYS, -1).astype(BF16)
    x1, h2, st = _outproj(x, attn, conv, wo[:ATTN_WIDTH], wo[ATTN_WIDTH:], gt1,
                          norm_ffn.reshape(1, D_MODEL), sc2, sh2, peer_wq.astype(BF16), keys, tm)
    planes = _topk(st, _tile(s, 256))
    return _peer(h2, _to_words(peer_u.astype(BF16)), _to_words(peer_v.T.astype(BF16)), planes, x1, gt2,
                 _tile(s, 512), SUBLANES * PEER_NKEYS)


def kernel(x, c, positions, w_ada, b_ada, norm_mix, norm_ffn, w_in, q_norm, k_norm, conv_w, w_out,
           peer_wq, peer_keys, peer_u, peer_v):
    b, s, d = x.shape
    assert b == 1 and d == D_MODEL and s % (max(DILATIONS) * BRANCH_KEYS) == 0, x.shape
    depth = w_ada.shape[0]
    mod = _ada(c, w_ada, b_ada)
    cos_t, sin_t = _rope_tables(positions, _tile(s, 1024))
    blk = jnp.arange(256) // HEAD_DIM
    bd = (blk[:, None] == blk[None, :]).astype(BF16)
    y = x.reshape(s, d)
    for l in range(depth):
        y = _layer(y, mod[l], cos_t, sin_t, bd, norm_mix[l], norm_ffn[l], w_in[l], q_norm[l], k_norm[l],
                   conv_w[l], w_out[l], peer_wq[l], peer_keys[l], peer_u[l], peer_v[l])
    return y.reshape(b, s, d)
```

```python
import functools

import jax
import jax.numpy as jnp
from jax import lax
from jax.experimental import pallas as pl
from jax.experimental.pallas import tpu as pltpu

F32 = jnp.float32
BF16 = jnp.bfloat16

D_MODEL = 1024
HEAD_DIM = 64
N_ATTN_HEADS = 12
ATTN_WIDTH = N_ATTN_HEADS * HEAD_DIM
CONV_WIDTH = D_MODEL - ATTN_WIDTH
IN_PROJ_WIDTH = 3 * ATTN_WIDTH + 3 * CONV_WIDTH
BRANCH_KEYS = 128
DILATIONS = (16, 4, 1)
ROPE_THETA = 10000.0
EPS = 1e-6
PEER_HEADS = 8
PEER_NKEYS = 128
PEER_TOPK = 16
N_MOD = 6

LANES = 128
SUBLANES = 8
VMEM_LIMIT = 56 * 1024 * 1024

NEG_INF = float("-inf")


def _dot(a, b):
    return jnp.dot(a, b, preferred_element_type=F32)


def _dot_nt(a, b):
    return lax.dot_general(a, b, (((1,), (1,)), ((), ())), preferred_element_type=F32)


def _gelu(x):
    return 0.5 * x * (1.0 + lax.erf(x * (0.5 ** 0.5)))


def _params(*sem):
    return pltpu.CompilerParams(dimension_semantics=sem, vmem_limit_bytes=VMEM_LIMIT)


def _ada_kernel(c_ref, w_ref, b_ref, o_ref):
    o_ref[...] = jnp.dot(c_ref[...], w_ref[...], preferred_element_type=F32,
                         precision=lax.Precision.HIGHEST) + b_ref[...]


def _ada(c, w_ada, b_ada):
    depth = w_ada.shape[0]
    c8 = jnp.broadcast_to(c, (SUBLANES, D_MODEL))
    out = pl.pallas_call(
        _ada_kernel,
        grid=(depth, N_MOD),
        in_specs=[pl.BlockSpec((SUBLANES, D_MODEL), lambda l, j: (0, 0)),
                  pl.BlockSpec((None, D_MODEL, D_MODEL), lambda l, j: (l, 0, j)),
                  pl.BlockSpec((None, 1, D_MODEL), lambda l, j: (l, 0, j))],
        out_specs=pl.BlockSpec((None, SUBLANES, D_MODEL), lambda l, j: (l, 0, j)),
        out_shape=jax.ShapeDtypeStruct((depth, SUBLANES, N_MOD * D_MODEL), F32),
        compiler_params=_params("parallel", "parallel"),
        name="ada_mod",
    )(c8, w_ada, b_ada.reshape(depth, 1, N_MOD * D_MODEL))
    return out[:, 0, :]


def _rope_kernel(pos_ref, freq_ref, sign_ref, cos_ref, sin_ref):
    ang = pos_ref[...].astype(F32) * freq_ref[...]
    cos_ref[...] = jnp.cos(ang)
    sin_ref[...] = jnp.sin(ang) * sign_ref[...]


def _rope_tables(positions, tm):
    s = positions.shape[-1]
    half = HEAD_DIM // 2
    freq = ROPE_THETA ** (-jnp.arange(half, dtype=F32) / half)
    freq = jnp.tile(freq, LANES // half).reshape(1, LANES)
    sign = jnp.tile(jnp.concatenate([-jnp.ones((half,), F32), jnp.ones((half,), F32)]), LANES // HEAD_DIM)
    sign = sign.reshape(1, LANES)
    row = pl.BlockSpec((1, LANES), lambda i: (0, 0))
    return pl.pallas_call(
        _rope_kernel,
        grid=(s // tm,),
        in_specs=[pl.BlockSpec((tm, 1), lambda i: (i, 0)), row, row],
        out_specs=[pl.BlockSpec((tm, LANES), lambda i: (i, 0))] * 2,
        out_shape=[jax.ShapeDtypeStruct((s, LANES), F32)] * 2,
        compiler_params=_params("parallel"),
        name="rope_tables",
    )(positions.reshape(s, 1), freq, sign)


def _rms_mod(x, g, sc, sh):
    ms = jnp.mean(x * x, axis=-1, keepdims=True)
    y = x * lax.rsqrt(ms + EPS) * g
    return y * (1.0 + sc) + sh


def _inproj_kernel(x_ref, g_ref, sc_ref, sh_ref, w_ref, qg_ref, kg_ref, cos_ref, sin_ref, bd_ref, cw_ref,
                   q_ref, k_ref, v_ref, conv_ref, ubuf_ref, *, tm):
    i = pl.program_id(0)
    h = _rms_mod(x_ref[...], g_ref[...], sc_ref[...], sh_ref[...])
    proj = _dot(h.astype(BF16), w_ref[...])
    cos = cos_ref[...]
    sin = sin_ref[...]
    lane = lax.broadcasted_iota(jnp.int32, (tm, LANES), 1)
    upper = (lane & (HEAD_DIM // 2)) != 0

    def head_norm_rope(z, gain_ref, out_ref, scale):
        for j in range(ATTN_WIDTH // 256):
            zj = z[:, 256 * j:256 * (j + 1)]
            ss = _dot((zj * zj).astype(BF16), bd_ref[...])
            zn = zj * lax.rsqrt(ss * (1.0 / HEAD_DIM) + EPS) * gain_ref[:, 256 * j:256 * (j + 1)]
            for t in range(2):
                zt = zn[:, LANES * t:LANES * (t + 1)]
                partner = jnp.where(upper, pltpu.roll(zt, HEAD_DIM // 2, 1),
                                    pltpu.roll(zt, LANES - HEAD_DIM // 2, 1))
                r = zt * cos + partner * sin
                c0 = 256 * j + LANES * t
                out_ref[:, c0:c0 + LANES] = (r * scale).astype(BF16)

    a = ATTN_WIDTH
    head_norm_rope(proj[:, 0:a], qg_ref, q_ref, HEAD_DIM ** -0.5)
    head_norm_rope(proj[:, a:2 * a], kg_ref, k_ref, 1.0)
    v_ref[...] = proj[:, 2 * a:3 * a].astype(BF16)

    c = CONV_WIDTH
    bg = proj[:, 3 * a:3 * a + c]
    cg = proj[:, 3 * a + c:3 * a + 2 * c]
    xv = proj[:, 3 * a + 2 * c:3 * a + 3 * c]
    u = cg * xv

    @pl.when(i == 0)
    def _():
        ubuf_ref[0:SUBLANES, :] = jnp.zeros((SUBLANES, c), F32)

    ubuf_ref[SUBLANES:SUBLANES + tm, :] = u
    u1 = ubuf_ref[SUBLANES - 1:SUBLANES - 1 + tm, :]
    u2 = ubuf_ref[SUBLANES - 2:SUBLANES - 2 + tm, :]
    y = u2 * cw_ref[0:1, :] + u1 * cw_ref[1:2, :] + u * cw_ref[2:3, :]
    conv_ref[...] = (bg * y).astype(BF16)
    ubuf_ref[0:SUBLANES, :] = ubuf_ref[tm:tm + SUBLANES, :]


def _inproj(x, g, sc, sh, w_in, qg, kg, cos_t, sin_t, bd, conv_w, tm):
    s = x.shape[0]
    row = lambda n: pl.BlockSpec((1, n), lambda i: (0, 0))
    tok = lambda n: pl.BlockSpec((tm, n), lambda i: (i, 0))
    return pl.pallas_call(
        functools.partial(_inproj_kernel, tm=tm),
        grid=(s // tm,),
        in_specs=[tok(D_MODEL), row(D_MODEL), row(D_MODEL), row(D_MODEL),
                  pl.BlockSpec((D_MODEL, IN_PROJ_WIDTH), lambda i: (0, 0)),
                  row(ATTN_WIDTH), row(ATTN_WIDTH), tok(LANES), tok(LANES),
                  pl.BlockSpec((256, 256), lambda i: (0, 0)),
                  pl.BlockSpec((3, CONV_WIDTH), lambda i: (0, 0))],
        out_specs=[tok(ATTN_WIDTH), tok(ATTN_WIDTH), tok(ATTN_WIDTH), tok(CONV_WIDTH)],
        out_shape=[jax.ShapeDtypeStruct((s, ATTN_WIDTH), BF16)] * 3
        + [jax.ShapeDtypeStruct((s, CONV_WIDTH), BF16)],
        scratch_shapes=[pltpu.VMEM((tm + SUBLANES, CONV_WIDTH), F32)],
        compiler_params=_params("arbitrary"),
        name="inproj",
    )(x, g, sc, sh, w_in, qg, kg, cos_t, sin_t, bd, conv_w)


def _attn_kernel(*refs, first, last):
    tq = BRANCH_KEYS
    if first:
        q_ref, kp_ref, kc_ref, vp_ref, vc_ref, acc_out, ml_out = refs
    elif last:
        q_ref, kp_ref, kc_ref, vp_ref, vc_ref, acc_in, ml_in, out_ref = refs
    else:
        q_ref, kp_ref, kc_ref, vp_ref, vc_ref, acc_in, ml_in, acc_out, ml_out = refs
    ib = pl.program_id(1)
    qi = lax.broadcasted_iota(jnp.int32, (tq, tq), 0)
    ki = lax.broadcasted_iota(jnp.int32, (tq, tq), 1)
    mask_c = ki <= qi
    mask_p = jnp.logical_and(ki >= qi, ib > 0)
    lane = ki
    ml_old = None if first else ml_in[...]
    heads = range(N_ATTN_HEADS)
    sls = [slice(HEAD_DIM * h, HEAD_DIM * (h + 1)) for h in heads]
    sp = [jnp.where(mask_p, _dot_nt(q_ref[:, sl], kp_ref[:, sl]), NEG_INF) for sl in sls]
    sc = [jnp.where(mask_c, _dot_nt(q_ref[:, sl], kc_ref[:, sl]), NEG_INF) for sl in sls]
    m_new = [jnp.max(jnp.maximum(a, b), axis=1, keepdims=True) for a, b in zip(sp, sc)]
    if not first:
        m_old = [jnp.max(jnp.where(lane == h, ml_old, NEG_INF), axis=1, keepdims=True) for h in heads]
        l_old = [jnp.sum(jnp.where(lane == 16 + h, ml_old, 0.0), axis=1, keepdims=True) for h in heads]
        m_new = [jnp.maximum(a, b) for a, b in zip(m_old, m_new)]
        alpha = [jnp.exp(a - b) for a, b in zip(m_old, m_new)]
    pp = [jnp.exp(s - m) for s, m in zip(sp, m_new)]
    pc = [jnp.exp(s - m) for s, m in zip(sc, m_new)]
    l_new = [jnp.sum(a + b, axis=1, keepdims=True) for a, b in zip(pp, pc)]
    acc = [_dot(a.astype(BF16), vp_ref[:, sl]) + _dot(b.astype(BF16), vc_ref[:, sl])
           for a, b, sl in zip(pp, pc, sls)]
    if not first:
        l_new = [a * lo + ln for a, lo, ln in zip(alpha, l_old, l_new)]
        acc = [a * acc_in[:, sl] + pv for a, sl, pv in zip(alpha, sls, acc)]
    if last:
        for sl, a, l in zip(sls, acc, l_new):
            out_ref[:, sl] = (a / l).astype(BF16)
    else:
        ml_new = jnp.zeros((tq, LANES), F32)
        for h, sl in zip(heads, sls):
            acc_out[:, sl] = acc[h]
            ml_new = jnp.where(lane == h, m_new[h], ml_new)
            ml_new = jnp.where(lane == 16 + h, l_new[h], ml_new)
        ml_out[...] = ml_new


def _attn_pass(q, k, v, state, dil, first, last):
    s = q.shape[0]
    n = s // dil
    tq = BRANCH_KEYS
    view = lambda a: a.reshape(n, dil * a.shape[1])
    cur = lambda w: pl.BlockSpec((tq, w), lambda r, ib: (ib, r))
    prev = lambda w: pl.BlockSpec((tq, w), lambda r, ib: (jnp.maximum(ib - 1, 0), r))
    aw = ATTN_WIDTH
    in_specs = [cur(aw), prev(aw), cur(aw), prev(aw), cur(aw)]
    args = [view(q), view(k), view(k), view(v), view(v)]
    if not first:
        in_specs += [cur(aw), cur(LANES)]
        args += [view(state[0]), view(state[1])]
    if last:
        out_specs = [cur(aw)]
        out_shape = [jax.ShapeDtypeStruct((n, dil * aw), BF16)]
    else:
        out_specs = [cur(aw), cur(LANES)]
        out_shape = [jax.ShapeDtypeStruct((n, dil * aw), F32), jax.ShapeDtypeStruct((n, dil * LANES), F32)]
    outs = pl.pallas_call(
        functools.partial(_attn_kernel, first=first, last=last),
        grid=(dil, n // tq),
        in_specs=in_specs, out_specs=out_specs, out_shape=out_shape,
        compiler_params=_params("parallel", "parallel"),
        name=f"attn_d{dil}",
    )(*args)
    return [o.reshape(s, -1) for o in outs]


def _attention(q, k, v):
    state = None
    for idx, dil in enumerate(DILATIONS):
        state = _attn_pass(q, k, v, state, dil, first=idx == 0, last=idx == len(DILATIONS) - 1)
    return state[0]


def _outproj_kernel(x_ref, attn_ref, conv_ref, woa_ref, woc_ref, gt_ref, g_ref, sc_ref, sh_ref, wq_ref,
                    keys_ref, x1_ref, h2_ref, st_ref):
    mix = _dot(attn_ref[...], woa_ref[...]) + _dot(conv_ref[...], woc_ref[...])
    x1 = x_ref[...] + gt_ref[...] * mix
    x1_ref[...] = x1
    h2 = _rms_mod(x1, g_ref[...], sc_ref[...], sh_ref[...]).astype(BF16)
    h2_ref[...] = h2
    qp = _dot(h2, wq_ref[...])
    for g in range(2 * PEER_HEADS):
        qg = qp[:, PEER_NKEYS * g:PEER_NKEYS * (g + 1)].astype(BF16)
        st_ref[PEER_NKEYS * g:PEER_NKEYS * (g + 1), :] = _dot_nt(keys_ref[g], qg)


def _outproj(x, attn, conv, wo_a, wo_c, gt, g, sc, sh, wq, keys, tm):
    s = x.shape[0]
    nq = wq.shape[1]
    row = lambda n: pl.BlockSpec((1, n), lambda i: (0, 0))
    tok = lambda n: pl.BlockSpec((tm, n), lambda i: (i, 0))
    full = lambda a: pl.BlockSpec(a.shape, lambda i: (0,) * a.ndim)
    return pl.pallas_call(
        _outproj_kernel,
        grid=(s // tm,),
        in_specs=[tok(D_MODEL), tok(ATTN_WIDTH), tok(CONV_WIDTH), full(wo_a), full(wo_c),
                  row(D_MODEL), row(D_MODEL), row(D_MODEL), row(D_MODEL), full(wq), full(keys)],
        out_specs=[tok(D_MODEL), tok(D_MODEL), pl.BlockSpec((nq, tm), lambda i: (0, i))],
        out_shape=[jax.ShapeDtypeStruct((s, D_MODEL), F32), jax.ShapeDtypeStruct((s, D_MODEL), BF16),
                   jax.ShapeDtypeStruct((nq, s), F32)],
        compiler_params=_params("parallel"),
        name="outproj",
    )(x, attn, conv, wo_a, wo_c, gt, g, sc, sh, wq, keys)


def _top16(ss, tb):
    n = ss[0].shape[0]
    row = lax.broadcasted_iota(jnp.int32, (n, tb), 0)
    row16 = lax.broadcasted_iota(jnp.int32, (PEER_TOPK, tb), 0)
    cur = list(ss)
    rank = [jnp.full((n, tb), PEER_TOPK, jnp.int32) for _ in ss]
    vals = [jnp.zeros((PEER_TOPK, tb), F32) for _ in ss]
    for kk in range(PEER_TOPK):
        m = [jnp.max(c, axis=0, keepdims=True) for c in cur]
        idx = [jnp.min(jnp.where(c == mm, row, n), axis=0, keepdims=True) for c, mm in zip(cur, m)]
        sel = [row == i for i in idx]
        rank = [jnp.where(sl, kk, r) for sl, r in zip(sel, rank)]
        cur = [jnp.where(sl, NEG_INF, c) for sl, c in zip(sel, cur)]
        vals = [jnp.where(row16 == kk, mm, v) for mm, v in zip(m, vals)]
    return vals, rank


def _topk_kernel(st_ref, rank2_ref, cnt_ref, e1_ref, e2_ref, *, tb):
    kk = PEER_TOPK
    s1 = st_ref[0:PEER_NKEYS, :]
    s2 = st_ref[PEER_NKEYS:2 * PEER_NKEYS, :]
    (v1, v2), (rank1, rank2) = _top16([s1, s2], tb)

    sub = lax.broadcasted_iota(jnp.int32, (SUBLANES, tb), 0)
    v2lo, v2hi = v2[0:SUBLANES, :], v2[SUBLANES:kk, :]
    groups = [(v1[0:1, :] + v2lo, sub, 0), (v1[0:1, :] + v2hi, sub + SUBLANES, 0),
              (v1[1:2, :] + v2lo, sub + kk, 1)]
    for a in range(2, SUBLANES):
        nb = kk // (a + 1)
        groups.append((jnp.where(sub < nb, v1[a:a + 1, :] + v2lo, NEG_INF), sub + kk * a, a))
    groups.append((v1[SUBLANES:kk, :] + v2[0:1, :], (sub + SUBLANES) * kk, None))
    cand = [g[0] for g in groups]
    flat = [g[1] for g in groups]
    cur = list(cand)
    big = kk * kk
    for _ in range(kk):
        m = functools.reduce(jnp.maximum, cur)
        m = jnp.max(m, axis=0, keepdims=True)
        hit = [jnp.where(c == m, f, big) for c, f in zip(cur, flat)]
        idx = jnp.min(functools.reduce(jnp.minimum, hit), axis=0, keepdims=True)
        cur = [jnp.where(f == idx, NEG_INF, c) for c, f in zip(cur, flat)]
    sel = [jnp.logical_and(c == NEG_INF, o != NEG_INF) for c, o in zip(cur, cand)]

    one = lambda mk: jnp.where(mk, 1.0, 0.0)
    counts = [jnp.sum(one(sel[0]) + one(sel[1]), axis=0, keepdims=True)]
    for gi in range(2, 2 + SUBLANES - 1):
        counts.append(jnp.sum(one(sel[gi]), axis=0, keepdims=True))
    tail = one(sel[-1])
    cnt = jnp.zeros((PEER_NKEYS, tb), F32)
    for a in range(kk):
        na = counts[a] if a < SUBLANES else tail[a - SUBLANES:a - SUBLANES + 1, :]
        cnt = jnp.where(rank1 == a, na, cnt)

    m1, m2 = v1[0:1, :], v2[0:1, :]
    mt = m1 + m2
    z = functools.reduce(lambda x, y: x + y,
                         [jnp.sum(jnp.where(sl, jnp.exp(o - mt), 0.0), axis=0, keepdims=True)
                          for sl, o in zip(sel, cand)])
    rank2_ref[...] = pltpu.bitcast(rank2.astype(F32).astype(BF16), jnp.uint32)
    cnt_ref[...] = cnt
    e1_ref[...] = jnp.exp(s1 - m1)
    e2_ref[...] = pltpu.bitcast((jnp.exp(s2 - m2) / z).astype(BF16), jnp.uint32)


def _topk(st, tb):
    s = st.shape[1]
    crow = PEER_NKEYS * SUBLANES // PACK
    plane = pl.BlockSpec((None, PEER_NKEYS, tb), lambda h, i: (h, 0, i))
    cplane = pl.BlockSpec((None, crow, tb), lambda h, i: (h, 0, i))
    shape = jax.ShapeDtypeStruct((PEER_HEADS, PEER_NKEYS, s), F32)
    cshape = jax.ShapeDtypeStruct((PEER_HEADS, crow, s), jnp.uint32)
    return pl.pallas_call(
        functools.partial(_topk_kernel, tb=tb),
        grid=(PEER_HEADS, s // tb),
        in_specs=[pl.BlockSpec((2 * PEER_NKEYS, tb), lambda h, i: (h, i))],
        out_specs=[cplane, plane, plane, cplane],
        out_shape=[cshape, shape, shape, cshape],
        compiler_params=_params("parallel", "parallel"),
        name="peer_topk",
    )(st)


PACK = 16


def _peer_kernel(h_ref, u_ref, vt_ref, rank2_ref, cnt_ref, e1_ref, e2_ref, x1_ref, gt_ref,
                 out_ref, acc_ref, p_ref, *, tm, ec):
    j = pl.program_id(1)
    rows = ec // PEER_NKEYS

    @pl.when(j == 0)
    def _():
        acc_ref[...] = jnp.zeros_like(acc_ref)

    crows = pl.ds(pl.multiple_of(j * rows, SUBLANES), rows)
    nt = tm // LANES
    er = ec // nt
    erw = er * SUBLANES // PACK
    zt = []
    for t in range(nt):
        u_t = pltpu.bitcast(u_ref[erw * t:erw * (t + 1), :], BF16)
        zt.append(_dot_nt(u_t, h_ref[...]))
        tl = slice(LANES * t, LANES * (t + 1))
        cnt_rows = [cnt_ref[hh, crows, tl] for hh in range(PEER_HEADS)]
        e1_rows = [e1_ref[hh, crows, tl] for hh in range(PEER_HEADS)]
        for r in range(rows):
            w = [None] * (PEER_NKEYS // PACK)
            for hh in range(PEER_HEADS):
                cb = jnp.broadcast_to(cnt_rows[hh][r:r + 1, :], (PACK, LANES)).astype(BF16)
                eb = jnp.broadcast_to(e1_rows[hh][r:r + 1, :], (PACK, LANES)).astype(BF16)
                for g in range(PEER_NKEYS // PACK):
                    gs = slice(SUBLANES * g, SUBLANES * (g + 1))
                    rk = pltpu.bitcast(rank2_ref[hh, gs, tl], BF16)
                    e2 = pltpu.bitcast(e2_ref[hh, gs, tl], BF16)
                    term = jnp.where(rk < cb, e2, 0.0) * eb
                    w[g] = term if w[g] is None else w[g] + term
            p_ref[PEER_NKEYS * r:PEER_NKEYS * (r + 1), tl] = jnp.concatenate(w, axis=0)
    for t in range(nt):
        rs = slice(er * t, er * (t + 1))
        p_ref[rs, :] = p_ref[rs, :] * _gelu(zt[t].astype(BF16))
    acc_ref[...] += _dot(pltpu.bitcast(vt_ref[...], BF16), p_ref[...])

    @pl.when(j == pl.num_programs(1) - 1)
    def _():
        out_ref[...] = x1_ref[...] + gt_ref[...] * acc_ref[...].T


def _peer(h2, u, vt, planes, x1, gt, tm, ec):
    s = h2.shape[0]
    ne = vt.shape[1]
    assert ec == SUBLANES * PEER_NKEYS, ec
    plane = pl.BlockSpec((PEER_HEADS, PEER_NKEYS, tm), lambda i, j: (0, 0, i))
    cplane = pl.BlockSpec((PEER_HEADS, PEER_NKEYS * SUBLANES // PACK, tm), lambda i, j: (0, 0, i))
    tok = pl.BlockSpec((tm, D_MODEL), lambda i, j: (i, 0))
    return pl.pallas_call(
        functools.partial(_peer_kernel, tm=tm, ec=ec),
        grid=(s // tm, ne // ec),
        in_specs=[tok, pl.BlockSpec((ec * SUBLANES // PACK, D_MODEL), lambda i, j: (j, 0)),
                  pl.BlockSpec((D_MODEL * SUBLANES // PACK, ec), lambda i, j: (0, j)),
                  cplane, plane, plane, cplane, tok, pl.BlockSpec((1, D_MODEL), lambda i, j: (0, 0))],
        out_specs=tok,
        out_shape=jax.ShapeDtypeStruct((s, D_MODEL), F32),
        scratch_shapes=[pltpu.VMEM((D_MODEL, tm), F32), pltpu.VMEM((ec, tm), BF16)],
        compiler_params=_params("parallel", "arbitrary"),
        name="peer_dense",
    )(h2, u, vt, *planes, x1, gt)


def _words_kernel(x_ref, o_ref, *, transpose):
    x = x_ref[...]
    if transpose:
        x = x.T
    o_ref[...] = pltpu.bitcast(x.astype(BF16), jnp.uint32)


def _expert_table_words(tab, transpose):
    ne, d = tab.shape
    blk = D_MODEL
    shrink = PACK // SUBLANES
    if transpose:
        out_shape, out_spec = (d // shrink, ne), pl.BlockSpec((d // shrink, blk), lambda i: (0, i))
    else:
        out_shape, out_spec = (ne // shrink, d), pl.BlockSpec((blk // shrink, d), lambda i: (i, 0))
    return pl.pallas_call(
        functools.partial(_words_kernel, transpose=transpose),
        grid=(ne // blk,),
        in_specs=[pl.BlockSpec((blk, d), lambda i: (i, 0))],
        out_specs=out_spec,
        out_shape=jax.ShapeDtypeStruct(out_shape, jnp.uint32),
        compiler_params=_params("parallel"),
        name="expert_words_t" if transpose else "expert_words",
    )(tab)


def _tile(s, want):
    t = min(want, s)
    assert s % t == 0, (s, t)
    return t


def _layer(x, mod, cos_t, sin_t, bd, norm_mix, norm_ffn, w_in, q_norm, k_norm, conv_w, w_out,
           peer_wq, peer_keys, peer_u, peer_v):
    s = x.shape[0]
    sh1, sc1, gt1, sh2, sc2, gt2 = [m.reshape(1, D_MODEL) for m in jnp.split(mod, N_MOD)]
    tm = _tile(s, 512)
    qg = jnp.tile(q_norm, N_ATTN_HEADS).reshape(1, ATTN_WIDTH)
    kg = jnp.tile(k_norm, N_ATTN_HEADS).reshape(1, ATTN_WIDTH)
    q, k, v, conv = _inproj(x, norm_mix.reshape(1, D_MODEL), sc1, sh1, w_in.astype(BF16), qg, kg,
                            cos_t, sin_t, bd, conv_w, tm)
    attn = _attention(q, k, v)
    wo = w_out.astype(BF16)
    keys = peer_keys.reshape(2 * PEER_HEADS, PEER_NKEYS, -1).astype(BF16)
    x1, h2, st = _outproj(x, attn, conv, wo[:ATTN_WIDTH], wo[ATTN_WIDTH:], gt1,
                          norm_ffn.reshape(1, D_MODEL), sc2, sh2, peer_wq.astype(BF16), keys, tm)
    planes = _topk(st, _tile(s, 512))
    return _peer(h2, _expert_table_words(peer_u, False), _expert_table_words(peer_v, True), planes, x1, gt2,
                 _tile(s, 512), SUBLANES * PEER_NKEYS)


def kernel(x, c, positions, w_ada, b_ada, norm_mix, norm_ffn, w_in, q_norm, k_norm, conv_w, w_out,
           peer_wq, peer_keys, peer_u, peer_v):
    b, s, d = x.shape
    assert b == 1 and d == D_MODEL and s % (max(DILATIONS) * BRANCH_KEYS) == 0, x.shape
    depth = w_ada.shape[0]
    mod = _ada(c, w_ada, b_ada)
    cos_t, sin_t = _rope_tables(positions, _tile(s, 1024))
    blk = jnp.arange(256) // HEAD_DIM
    bd = (blk[:, None] == blk[None, :]).astype(BF16)
    y = x.reshape(s, d)
    for l in range(depth):
        y = _layer(y, mod[l], cos_t, sin_t, bd, norm_mix[l], norm_ffn[l], w_in[l], q_norm[l], k_norm[l],
                   conv_w[l], w_out[l], peer_wq[l], peer_keys[l], peer_u[l], peer_v[l])
    return y.reshape(b, s, d)
```

```python
import functools

import jax
import jax.numpy as jnp
from jax import lax
from jax.experimental import pallas as pl
from jax.experimental.pallas import tpu as pltpu

F32 = jnp.float32
BF16 = jnp.bfloat16

D_MODEL = 1024
HEAD_DIM = 64
N_ATTN_HEADS = 12
ATTN_WIDTH = N_ATTN_HEADS * HEAD_DIM
CONV_WIDTH = D_MODEL - ATTN_WIDTH
IN_PROJ_WIDTH = 3 * ATTN_WIDTH + 3 * CONV_WIDTH
BRANCH_KEYS = 128
DILATIONS = (16, 4, 1)
ROPE_THETA = 10000.0
EPS = 1e-6
PEER_HEADS = 8
PEER_NKEYS = 128
PEER_TOPK = 16
N_MOD = 6

LANES = 128
SUBLANES = 8
VMEM_LIMIT = 56 * 1024 * 1024

NEG_INF = float("-inf")


def _dot(a, b):
    return jnp.dot(a, b, preferred_element_type=F32)


def _dot_nt(a, b):
    return lax.dot_general(a, b, (((1,), (1,)), ((), ())), preferred_element_type=F32)


def _gelu(x):
    return 0.5 * x * (1.0 + lax.erf(x * (0.5 ** 0.5)))


def _params(*sem):
    return pltpu.CompilerParams(dimension_semantics=sem, vmem_limit_bytes=VMEM_LIMIT)


def _ada_kernel(c_ref, w_ref, b_ref, o_ref):
    o_ref[...] = jnp.dot(c_ref[...], w_ref[...], preferred_element_type=F32,
                         precision=lax.Precision.HIGHEST) + b_ref[...]


def _ada(c, w_ada, b_ada):
    depth = w_ada.shape[0]
    c8 = jnp.broadcast_to(c, (SUBLANES, D_MODEL))
    out = pl.pallas_call(
        _ada_kernel,
        grid=(depth, N_MOD),
        in_specs=[pl.BlockSpec((SUBLANES, D_MODEL), lambda l, j: (0, 0)),
                  pl.BlockSpec((None, D_MODEL, D_MODEL), lambda l, j: (l, 0, j)),
                  pl.BlockSpec((None, 1, D_MODEL), lambda l, j: (l, 0, j))],
        out_specs=pl.BlockSpec((None, SUBLANES, D_MODEL), lambda l, j: (l, 0, j)),
        out_shape=jax.ShapeDtypeStruct((depth, SUBLANES, N_MOD * D_MODEL), F32),
        compiler_params=_params("parallel", "parallel"),
        name="ada_mod",
    )(c8, w_ada, b_ada.reshape(depth, 1, N_MOD * D_MODEL))
    return out[:, 0, :]


def _rope_kernel(pos_ref, freq_ref, sign_ref, cos_ref, sin_ref):
    ang = pos_ref[...].astype(F32) * freq_ref[...]
    cos_ref[...] = jnp.cos(ang)
    sin_ref[...] = jnp.sin(ang) * sign_ref[...]


def _rope_tables(positions, tm):
    s = positions.shape[-1]
    half = HEAD_DIM // 2
    freq = ROPE_THETA ** (-jnp.arange(half, dtype=F32) / half)
    freq = jnp.tile(freq, LANES // half).reshape(1, LANES)
    sign = jnp.tile(jnp.concatenate([-jnp.ones((half,), F32), jnp.ones((half,), F32)]), LANES // HEAD_DIM)
    sign = sign.reshape(1, LANES)
    row = pl.BlockSpec((1, LANES), lambda i: (0, 0))
    return pl.pallas_call(
        _rope_kernel,
        grid=(s // tm,),
        in_specs=[pl.BlockSpec((tm, 1), lambda i: (i, 0)), row, row],
        out_specs=[pl.BlockSpec((tm, LANES), lambda i: (i, 0))] * 2,
        out_shape=[jax.ShapeDtypeStruct((s, LANES), F32)] * 2,
        compiler_params=_params("parallel"),
        name="rope_tables",
    )(positions.reshape(s, 1), freq, sign)


def _rms_mod(x, g, sc, sh):
    ms = jnp.mean(x * x, axis=-1, keepdims=True)
    y = x * lax.rsqrt(ms + EPS) * g
    return y * (1.0 + sc) + sh


def _inproj_kernel(x_ref, g_ref, sc_ref, sh_ref, w_ref, qg_ref, kg_ref, cos_ref, sin_ref, bd_ref, cw_ref,
                   q_ref, k_ref, v_ref, conv_ref, ubuf_ref, *, tm):
    i = pl.program_id(0)
    h = _rms_mod(x_ref[...], g_ref[...], sc_ref[...], sh_ref[...])
    proj = _dot(h.astype(BF16), w_ref[...])
    cos = cos_ref[...]
    sin = sin_ref[...]
    lane = lax.broadcasted_iota(jnp.int32, (tm, LANES), 1)
    upper = (lane & (HEAD_DIM // 2)) != 0

    def head_norm_rope(z, gain_ref, out_ref, scale):
        for j in range(ATTN_WIDTH // 256):
            zj = z[:, 256 * j:256 * (j + 1)]
            ss = _dot((zj * zj).astype(BF16), bd_ref[...])
            zn = zj * lax.rsqrt(ss * (1.0 / HEAD_DIM) + EPS) * gain_ref[:, 256 * j:256 * (j + 1)]
            for t in range(2):
                zt = zn[:, LANES * t:LANES * (t + 1)]
                partner = jnp.where(upper, pltpu.roll(zt, HEAD_DIM // 2, 1),
                                    pltpu.roll(zt, LANES - HEAD_DIM // 2, 1))
                r = zt * cos + partner * sin
                c0 = 256 * j + LANES * t
                out_ref[:, c0:c0 + LANES] = (r * scale).astype(BF16)

    a = ATTN_WIDTH
    head_norm_rope(proj[:, 0:a], qg_ref, q_ref, HEAD_DIM ** -0.5)
    head_norm_rope(proj[:, a:2 * a], kg_ref, k_ref, 1.0)
    v_ref[...] = proj[:, 2 * a:3 * a].astype(BF16)

    c = CONV_WIDTH
    bg = proj[:, 3 * a:3 * a + c]
    cg = proj[:, 3 * a + c:3 * a + 2 * c]
    xv = proj[:, 3 * a + 2 * c:3 * a + 3 * c]
    u = cg * xv

    @pl.when(i == 0)
    def _():
        ubuf_ref[0:SUBLANES, :] = jnp.zeros((SUBLANES, c), F32)

    ubuf_ref[SUBLANES:SUBLANES + tm, :] = u
    u1 = ubuf_ref[SUBLANES - 1:SUBLANES - 1 + tm, :]
    u2 = ubuf_ref[SUBLANES - 2:SUBLANES - 2 + tm, :]
    y = u2 * cw_ref[0:1, :] + u1 * cw_ref[1:2, :] + u * cw_ref[2:3, :]
    conv_ref[...] = (bg * y).astype(BF16)
    ubuf_ref[0:SUBLANES, :] = ubuf_ref[tm:tm + SUBLANES, :]


def _inproj(x, g, sc, sh, w_in, qg, kg, cos_t, sin_t, bd, conv_w, tm):
    s = x.shape[0]
    row = lambda n: pl.BlockSpec((1, n), lambda i: (0, 0))
    tok = lambda n: pl.BlockSpec((tm, n), lambda i: (i, 0))
    return pl.pallas_call(
        functools.partial(_inproj_kernel, tm=tm),
        grid=(s // tm,),
        in_specs=[tok(D_MODEL), row(D_MODEL), row(D_MODEL), row(D_MODEL),
                  pl.BlockSpec((D_MODEL, IN_PROJ_WIDTH), lambda i: (0, 0)),
                  row(ATTN_WIDTH), row(ATTN_WIDTH), tok(LANES), tok(LANES),
                  pl.BlockSpec((256, 256), lambda i: (0, 0)),
                  pl.BlockSpec((3, CONV_WIDTH), lambda i: (0, 0))],
        out_specs=[tok(ATTN_WIDTH), tok(ATTN_WIDTH), tok(ATTN_WIDTH), tok(CONV_WIDTH)],
        out_shape=[jax.ShapeDtypeStruct((s, ATTN_WIDTH), BF16)] * 3
        + [jax.ShapeDtypeStruct((s, CONV_WIDTH), BF16)],
        scratch_shapes=[pltpu.VMEM((tm + SUBLANES, CONV_WIDTH), F32)],
        compiler_params=_params("arbitrary"),
        name="inproj",
    )(x, g, sc, sh, w_in, qg, kg, cos_t, sin_t, bd, conv_w)


def _attn_kernel(*refs, first, last):
    tq = BRANCH_KEYS
    if first:
        q_ref, kp_ref, kc_ref, vp_ref, vc_ref, acc_out, ml_out = refs
    elif last:
        q_ref, kp_ref, kc_ref, vp_ref, vc_ref, acc_in, ml_in, out_ref = refs
    else:
        q_ref, kp_ref, kc_ref, vp_ref, vc_ref, acc_in, ml_in, acc_out, ml_out = refs
    ib = pl.program_id(1)
    qi = lax.broadcasted_iota(jnp.int32, (tq, tq), 0)
    ki = lax.broadcasted_iota(jnp.int32, (tq, tq), 1)
    mask_c = ki <= qi
    mask_p = jnp.logical_and(ki >= qi, ib > 0)
    lane = ki
    ml_old = None if first else ml_in[...]
    heads = range(N_ATTN_HEADS)
    sls = [slice(HEAD_DIM * h, HEAD_DIM * (h + 1)) for h in heads]
    sp = [jnp.where(mask_p, _dot_nt(q_ref[:, sl], kp_ref[:, sl]), NEG_INF) for sl in sls]
    sc = [jnp.where(mask_c, _dot_nt(q_ref[:, sl], kc_ref[:, sl]), NEG_INF) for sl in sls]
    m_new = [jnp.max(jnp.maximum(a, b), axis=1, keepdims=True) for a, b in zip(sp, sc)]
    if not first:
        m_old = [jnp.max(jnp.where(lane == h, ml_old, NEG_INF), axis=1, keepdims=True) for h in heads]
        l_old = [jnp.sum(jnp.where(lane == 16 + h, ml_old, 0.0), axis=1, keepdims=True) for h in heads]
        m_new = [jnp.maximum(a, b) for a, b in zip(m_old, m_new)]
        alpha = [jnp.exp(a - b) for a, b in zip(m_old, m_new)]
    pp = [jnp.exp(s - m) for s, m in zip(sp, m_new)]
    pc = [jnp.exp(s - m) for s, m in zip(sc, m_new)]
    l_new = [jnp.sum(a + b, axis=1, keepdims=True) for a, b in zip(pp, pc)]
    acc = [_dot(a.astype(BF16), vp_ref[:, sl]) + _dot(b.astype(BF16), vc_ref[:, sl])
           for a, b, sl in zip(pp, pc, sls)]
    if not first:
        l_new = [a * lo + ln for a, lo, ln in zip(alpha, l_old, l_new)]
        acc = [a * acc_in[:, sl] + pv for a, sl, pv in zip(alpha, sls, acc)]
    if last:
        for sl, a, l in zip(sls, acc, l_new):
            out_ref[:, sl] = (a / l).astype(BF16)
    else:
        ml_new = jnp.zeros((tq, LANES), F32)
        for h, sl in zip(heads, sls):
            acc_out[:, sl] = acc[h]
            ml_new = jnp.where(lane == h, m_new[h], ml_new)
            ml_new = jnp.where(lane == 16 + h, l_new[h], ml_new)
        ml_out[...] = ml_new


def _attn_pass(q, k, v, state, dil, first, last):
    s = q.shape[0]
    n = s // dil
    tq = BRANCH_KEYS
    view = lambda a: a.reshape(n, dil * a.shape[1])
    cur = lambda w: pl.BlockSpec((tq, w), lambda r, ib: (ib, r))
    prev = lambda w: pl.BlockSpec((tq, w), lambda r, ib: (jnp.maximum(ib - 1, 0), r))
    aw = ATTN_WIDTH
    in_specs = [cur(aw), prev(aw), cur(aw), prev(aw), cur(aw)]
    args = [view(q), view(k), view(k), view(v), view(v)]
    if not first:
        in_specs += [cur(aw), cur(LANES)]
        args += [view(state[0]), view(state[1])]
    if last:
        out_specs = [cur(aw)]
        out_shape = [jax.ShapeDtypeStruct((n, dil * aw), BF16)]
    else:
        out_specs = [cur(aw), cur(LANES)]
        out_shape = [jax.ShapeDtypeStruct((n, dil * aw), F32), jax.ShapeDtypeStruct((n, dil * LANES), F32)]
    outs = pl.pallas_call(
        functools.partial(_attn_kernel, first=first, last=last),
        grid=(dil, n // tq),
        in_specs=in_specs, out_specs=out_specs, out_shape=out_shape,
        compiler_params=_params("parallel", "parallel"),
        name=f"attn_d{dil}",
    )(*args)
    return [o.reshape(s, -1) for o in outs]


def _attention(q, k, v):
    state = None
    for idx, dil in enumerate(DILATIONS):
        state = _attn_pass(q, k, v, state, dil, first=idx == 0, last=idx == len(DILATIONS) - 1)
    return state[0]


def _outproj_kernel(x_ref, attn_ref, conv_ref, woa_ref, woc_ref, gt_ref, g_ref, sc_ref, sh_ref, wq_ref,
                    keys_ref, x1_ref, h2_ref, st_ref):
    mix = _dot(attn_ref[...], woa_ref[...]) + _dot(conv_ref[...], woc_ref[...])
    x1 = x_ref[...] + gt_ref[...] * mix
    x1_ref[...] = x1
    h2 = _rms_mod(x1, g_ref[...], sc_ref[...], sh_ref[...]).astype(BF16)
    h2_ref[...] = h2
    qp = _dot(h2, wq_ref[...])
    for g in range(2 * PEER_HEADS):
        qg = qp[:, PEER_NKEYS * g:PEER_NKEYS * (g + 1)].astype(BF16)
        st_ref[PEER_NKEYS * g:PEER_NKEYS * (g + 1), :] = _dot_nt(keys_ref[g], qg)


def _outproj(x, attn, conv, wo_a, wo_c, gt, g, sc, sh, wq, keys, tm):
    s = x.shape[0]
    nq = wq.shape[1]
    row = lambda n: pl.BlockSpec((1, n), lambda i: (0, 0))
    tok = lambda n: pl.BlockSpec((tm, n), lambda i: (i, 0))
    full = lambda a: pl.BlockSpec(a.shape, lambda i: (0,) * a.ndim)
    return pl.pallas_call(
        _outproj_kernel,
        grid=(s // tm,),
        in_specs=[tok(D_MODEL), tok(ATTN_WIDTH), tok(CONV_WIDTH), full(wo_a), full(wo_c),
                  row(D_MODEL), row(D_MODEL), row(D_MODEL), row(D_MODEL), full(wq), full(keys)],
        out_specs=[tok(D_MODEL), tok(D_MODEL), pl.BlockSpec((nq, tm), lambda i: (0, i))],
        out_shape=[jax.ShapeDtypeStruct((s, D_MODEL), F32), jax.ShapeDtypeStruct((s, D_MODEL), BF16),
                   jax.ShapeDtypeStruct((nq, s), F32)],
        compiler_params=_params("parallel"),
        name="outproj",
    )(x, attn, conv, wo_a, wo_c, gt, g, sc, sh, wq, keys)


def _top16(ss, tb):
    n = ss[0].shape[0]
    row = lax.broadcasted_iota(jnp.int32, (n, tb), 0)
    row16 = lax.broadcasted_iota(jnp.int32, (PEER_TOPK, tb), 0)
    cur = list(ss)
    rank = [jnp.full((n, tb), PEER_TOPK, jnp.int32) for _ in ss]
    vals = [jnp.zeros((PEER_TOPK, tb), F32) for _ in ss]
    for kk in range(PEER_TOPK):
        m = [jnp.max(c, axis=0, keepdims=True) for c in cur]
        idx = [jnp.min(jnp.where(c == mm, row, n), axis=0, keepdims=True) for c, mm in zip(cur, m)]
        sel = [row == i for i in idx]
        rank = [jnp.where(sl, kk, r) for sl, r in zip(sel, rank)]
        cur = [jnp.where(sl, NEG_INF, c) for sl, c in zip(sel, cur)]
        vals = [jnp.where(row16 == kk, mm, v) for mm, v in zip(m, vals)]
    return vals, rank


def _topk_kernel(st_ref, rank2_ref, cnt_ref, e1_ref, e2_ref, *, tb):
    kk = PEER_TOPK
    s1 = st_ref[0:PEER_NKEYS, :]
    s2 = st_ref[PEER_NKEYS:2 * PEER_NKEYS, :]
    (v1, v2), (rank1, rank2) = _top16([s1, s2], tb)

    sub = lax.broadcasted_iota(jnp.int32, (SUBLANES, tb), 0)
    v2lo, v2hi = v2[0:SUBLANES, :], v2[SUBLANES:kk, :]
    groups = [(v1[0:1, :] + v2lo, sub, 0), (v1[0:1, :] + v2hi, sub + SUBLANES, 0),
              (v1[1:2, :] + v2lo, sub + kk, 1)]
    for a in range(2, SUBLANES):
        nb = kk // (a + 1)
        groups.append((jnp.where(sub < nb, v1[a:a + 1, :] + v2lo, NEG_INF), sub + kk * a, a))
    groups.append((v1[SUBLANES:kk, :] + v2[0:1, :], (sub + SUBLANES) * kk, None))
    cand = [g[0] for g in groups]
    flat = [g[1] for g in groups]
    cur = list(cand)
    big = kk * kk
    for _ in range(kk):
        m = functools.reduce(jnp.maximum, cur)
        m = jnp.max(m, axis=0, keepdims=True)
        hit = [jnp.where(c == m, f, big) for c, f in zip(cur, flat)]
        idx = jnp.min(functools.reduce(jnp.minimum, hit), axis=0, keepdims=True)
        cur = [jnp.where(f == idx, NEG_INF, c) for c, f in zip(cur, flat)]
    sel = [jnp.logical_and(c == NEG_INF, o != NEG_INF) for c, o in zip(cur, cand)]

    one = lambda mk: jnp.where(mk, 1.0, 0.0)
    counts = [jnp.sum(one(sel[0]) + one(sel[1]), axis=0, keepdims=True)]
    for gi in range(2, 2 + SUBLANES - 1):
        counts.append(jnp.sum(one(sel[gi]), axis=0, keepdims=True))
    tail = one(sel[-1])
    cnt = jnp.zeros((PEER_NKEYS, tb), F32)
    for a in range(kk):
        na = counts[a] if a < SUBLANES else tail[a - SUBLANES:a - SUBLANES + 1, :]
        cnt = jnp.where(rank1 == a, na, cnt)

    m1, m2 = v1[0:1, :], v2[0:1, :]
    mt = m1 + m2
    z = functools.reduce(lambda x, y: x + y,
                         [jnp.sum(jnp.where(sl, jnp.exp(o - mt), 0.0), axis=0, keepdims=True)
                          for sl, o in zip(sel, cand)])
    rank2_ref[...] = pltpu.bitcast(rank2.astype(F32).astype(BF16), jnp.uint32)
    cnt_ref[...] = cnt
    e1_ref[...] = jnp.exp(s1 - m1)
    e2_ref[...] = pltpu.bitcast((jnp.exp(s2 - m2) / z).astype(BF16), jnp.uint32)


def _topk(st, tb):
    s = st.shape[1]
    crow = PEER_NKEYS * SUBLANES // PACK
    plane = pl.BlockSpec((None, PEER_NKEYS, tb), lambda h, i: (h, 0, i))
    cplane = pl.BlockSpec((None, crow, tb), lambda h, i: (h, 0, i))
    shape = jax.ShapeDtypeStruct((PEER_HEADS, PEER_NKEYS, s), F32)
    cshape = jax.ShapeDtypeStruct((PEER_HEADS, crow, s), jnp.uint32)
    return pl.pallas_call(
        functools.partial(_topk_kernel, tb=tb),
        grid=(PEER_HEADS, s // tb),
        in_specs=[pl.BlockSpec((2 * PEER_NKEYS, tb), lambda h, i: (h, i))],
        out_specs=[cplane, plane, plane, cplane],
        out_shape=[cshape, shape, shape, cshape],
        compiler_params=_params("parallel", "parallel"),
        name="peer_topk",
    )(st)


PACK = 16
PEER_PIECES = 4


def _peer_kernel(h_ref, u_ref, vt_ref, rank2_ref, cnt_ref, e1_ref, e2_ref, x1_ref, gt_ref,
                 out_ref, acc_ref, p_ref, *, tm, ec):
    j = pl.program_id(1)
    rows = ec // PEER_NKEYS

    @pl.when(j == 0)
    def _():
        acc_ref[...] = jnp.zeros_like(acc_ref)

    crows = pl.ds(pl.multiple_of(j * rows, SUBLANES), rows)
    nt = tm // LANES
    tiles_per_piece = nt // PEER_PIECES
    er = ec // PEER_PIECES
    erw = er * SUBLANES // PACK
    zt = []
    for t in range(nt):
        if t % tiles_per_piece == 0:
            q = t // tiles_per_piece
            u_q = pltpu.bitcast(u_ref[erw * q:erw * (q + 1), :], BF16)
            zt.append(_dot_nt(u_q, h_ref[...]))
        tl = slice(LANES * t, LANES * (t + 1))
        cnt_rows = [cnt_ref[hh, crows, tl] for hh in range(PEER_HEADS)]
        e1_rows = [e1_ref[hh, crows, tl] for hh in range(PEER_HEADS)]
        for r in range(rows):
            w = [None] * (PEER_NKEYS // PACK)
            for hh in range(PEER_HEADS):
                cb = jnp.broadcast_to(cnt_rows[hh][r:r + 1, :], (PACK, LANES)).astype(BF16)
                eb = jnp.broadcast_to(e1_rows[hh][r:r + 1, :], (PACK, LANES)).astype(BF16)
                for g in range(PEER_NKEYS // PACK):
                    gs = slice(SUBLANES * g, SUBLANES * (g + 1))
                    rk = pltpu.bitcast(rank2_ref[hh, gs, tl], BF16)
                    e2 = pltpu.bitcast(e2_ref[hh, gs, tl], BF16)
                    term = jnp.where(rk < cb, e2, 0.0) * eb
                    w[g] = term if w[g] is None else w[g] + term
            p_ref[PEER_NKEYS * r:PEER_NKEYS * (r + 1), tl] = jnp.concatenate(w, axis=0)
    for q in range(PEER_PIECES):
        rs = slice(er * q, er * (q + 1))
        p_ref[rs, :] = p_ref[rs, :] * _gelu(zt[q].astype(BF16))
    acc_ref[...] += _dot(pltpu.bitcast(vt_ref[...], BF16), p_ref[...])

    @pl.when(j == pl.num_programs(1) - 1)
    def _():
        out_ref[...] = x1_ref[...] + gt_ref[...] * acc_ref[...].T


def _peer(h2, u, vt, planes, x1, gt, tm, ec):
    s = h2.shape[0]
    ne = vt.shape[1]
    assert ec == SUBLANES * PEER_NKEYS and (tm // LANES) % PEER_PIECES == 0, (ec, tm)
    once = dict(pipeline_mode=pl.Buffered(1))
    plane = pl.BlockSpec((PEER_HEADS, PEER_NKEYS, tm), lambda i, j: (0, 0, i), **once)
    cplane = pl.BlockSpec((PEER_HEADS, PEER_NKEYS * SUBLANES // PACK, tm), lambda i, j: (0, 0, i), **once)
    tok = pl.BlockSpec((tm, D_MODEL), lambda i, j: (i, 0))
    tok_in = pl.BlockSpec((tm, D_MODEL), lambda i, j: (i, 0), **once)
    return pl.pallas_call(
        functools.partial(_peer_kernel, tm=tm, ec=ec),
        grid=(s // tm, ne // ec),
        in_specs=[tok_in, pl.BlockSpec((ec * SUBLANES // PACK, D_MODEL), lambda i, j: (j, 0)),
                  pl.BlockSpec((D_MODEL * SUBLANES // PACK, ec), lambda i, j: (0, j)),
                  cplane, plane, plane, cplane, tok_in, pl.BlockSpec((1, D_MODEL), lambda i, j: (0, 0))],
        out_specs=tok,
        out_shape=jax.ShapeDtypeStruct((s, D_MODEL), F32),
        scratch_shapes=[pltpu.VMEM((D_MODEL, tm), F32), pltpu.VMEM((ec, tm), BF16)],
        compiler_params=_params("parallel", "arbitrary"),
        name="peer_dense",
    )(h2, u, vt, *planes, x1, gt)


def _words_kernel(x_ref, o_ref, *, transpose):
    x = x_ref[...]
    if transpose:
        x = x.T
    o_ref[...] = pltpu.bitcast(x.astype(BF16), jnp.uint32)


def _expert_table_words(tabs, layer, transpose):
    _, ne, d = tabs.shape
    blk = D_MODEL
    shrink = PACK // SUBLANES
    if transpose:
        out_shape, out_spec = (d // shrink, ne), pl.BlockSpec((d // shrink, blk), lambda i: (0, i))
    else:
        out_shape, out_spec = (ne // shrink, d), pl.BlockSpec((blk // shrink, d), lambda i: (i, 0))
    return pl.pallas_call(
        functools.partial(_words_kernel, transpose=transpose),
        grid=(ne // blk,),
        in_specs=[pl.BlockSpec((None, blk, d), lambda i: (layer, i, 0))],
        out_specs=out_spec,
        out_shape=jax.ShapeDtypeStruct(out_shape, jnp.uint32),
        compiler_params=_params("parallel"),
        name="expert_words_t" if transpose else "expert_words",
    )(tabs)


def _tile(s, want):
    t = min(want, s)
    assert s % t == 0, (s, t)
    return t


def _layer(x, mod, cos_t, sin_t, bd, norm_mix, norm_ffn, w_in, q_norm, k_norm, conv_w, w_out,
           peer_wq, peer_keys, peer_u_all, peer_v_all, layer):
    s = x.shape[0]
    sh1, sc1, gt1, sh2, sc2, gt2 = [m.reshape(1, D_MODEL) for m in jnp.split(mod, N_MOD)]
    tm = _tile(s, 512)
    qg = jnp.tile(q_norm, N_ATTN_HEADS).reshape(1, ATTN_WIDTH)
    kg = jnp.tile(k_norm, N_ATTN_HEADS).reshape(1, ATTN_WIDTH)
    q, k, v, conv = _inproj(x, norm_mix.reshape(1, D_MODEL), sc1, sh1, w_in.astype(BF16), qg, kg,
                            cos_t, sin_t, bd, conv_w, tm)
    attn = _attention(q, k, v)
    wo = w_out.astype(BF16)
    keys = peer_keys.reshape(2 * PEER_HEADS, PEER_NKEYS, -1).astype(BF16)
    x1, h2, st = _outproj(x, attn, conv, wo[:ATTN_WIDTH], wo[ATTN_WIDTH:], gt1,
                          norm_ffn.reshape(1, D_MODEL), sc2, sh2, peer_wq.astype(BF16), keys, tm)
    planes = _topk(st, _tile(s, 512))
    return _peer(h2, _expert_table_words(peer_u_all, layer, False), _expert_table_words(peer_v_all, layer, True),
                 planes, x1, gt2, _tile(s, 1024), SUBLANES * PEER_NKEYS)


def kernel(x, c, positions, w_ada, b_ada, norm_mix, norm_ffn, w_in, q_norm, k_norm, conv_w, w_out,
           peer_wq, peer_keys, peer_u, peer_v):
    b, s, d = x.shape
    assert b == 1 and d == D_MODEL and s % (max(DILATIONS) * BRANCH_KEYS) == 0, x.shape
    depth = w_ada.shape[0]
    mod = _ada(c, w_ada, b_ada)
    cos_t, sin_t = _rope_tables(positions, _tile(s, 1024))
    blk = jnp.arange(256) // HEAD_DIM
    bd = (blk[:, None] == blk[None, :]).astype(BF16)
    y = x.reshape(s, d)
    for l in range(depth):
        y = _layer(y, mod[l], cos_t, sin_t, bd, norm_mix[l], norm_ffn[l], w_in[l], q_norm[l], k_norm[l],
                   conv_w[l], w_out[l], peer_wq[l], peer_keys[l], peer_u, peer_v, l)
    return y.reshape(b, s, d)
```

```python
import functools

import jax
import jax.numpy as jnp
from jax import lax
from jax.experimental import pallas as pl
from jax.experimental.pallas import tpu as pltpu

F32 = jnp.float32
BF16 = jnp.bfloat16

D_MODEL = 1024
HEAD_DIM = 64
N_ATTN_HEADS = 12
ATTN_WIDTH = N_ATTN_HEADS * HEAD_DIM
CONV_WIDTH = D_MODEL - ATTN_WIDTH
IN_PROJ_WIDTH = 3 * ATTN_WIDTH + 3 * CONV_WIDTH
BRANCH_KEYS = 128
DILATIONS = (16, 4, 1)
ROPE_THETA = 10000.0
EPS = 1e-6
PEER_HEADS = 8
PEER_NKEYS = 128
PEER_TOPK = 16
N_MOD = 6

LANES = 128
SUBLANES = 8
VMEM_LIMIT = 56 * 1024 * 1024

NEG_INF = float("-inf")
F32_MAX = float(jnp.finfo(jnp.float32).max)
F32_MAX_ULP = 2.0 ** 104


def _dot(a, b):
    return jnp.dot(a, b, preferred_element_type=F32)


def _dot_nt(a, b):
    return lax.dot_general(a, b, (((1,), (1,)), ((), ())), preferred_element_type=F32)


def _gelu(x):
    return 0.5 * x * (1.0 + lax.erf(x * (0.5 ** 0.5)))


def _params(*sem):
    return pltpu.CompilerParams(dimension_semantics=sem, vmem_limit_bytes=VMEM_LIMIT)


def _ada_kernel(c_ref, w_ref, b_ref, o_ref):
    o_ref[...] = jnp.dot(c_ref[...], w_ref[...], preferred_element_type=F32,
                         precision=lax.Precision.HIGHEST) + b_ref[...]


def _ada(c, w_ada, b_ada):
    depth = w_ada.shape[0]
    c8 = jnp.broadcast_to(c, (SUBLANES, D_MODEL))
    out = pl.pallas_call(
        _ada_kernel,
        grid=(depth, N_MOD),
        in_specs=[pl.BlockSpec((SUBLANES, D_MODEL), lambda l, j: (0, 0)),
                  pl.BlockSpec((None, D_MODEL, D_MODEL), lambda l, j: (l, 0, j)),
                  pl.BlockSpec((None, 1, D_MODEL), lambda l, j: (l, 0, j))],
        out_specs=pl.BlockSpec((None, SUBLANES, D_MODEL), lambda l, j: (l, 0, j)),
        out_shape=jax.ShapeDtypeStruct((depth, SUBLANES, N_MOD * D_MODEL), F32),
        compiler_params=_params("parallel", "parallel"),
        name="ada_mod",
    )(c8, w_ada, b_ada.reshape(depth, 1, N_MOD * D_MODEL))
    return out[:, 0, :]


def _rope_kernel(pos_ref, freq_ref, sign_ref, cos_ref, sin_ref):
    ang = pos_ref[...].astype(F32) * freq_ref[...]
    cos_ref[...] = jnp.cos(ang)
    sin_ref[...] = jnp.sin(ang) * sign_ref[...]


def _rope_tables(positions, tm):
    s = positions.shape[-1]
    half = HEAD_DIM // 2
    freq = ROPE_THETA ** (-jnp.arange(half, dtype=F32) / half)
    freq = jnp.tile(freq, LANES // half).reshape(1, LANES)
    sign = jnp.tile(jnp.concatenate([-jnp.ones((half,), F32), jnp.ones((half,), F32)]), LANES // HEAD_DIM)
    sign = sign.reshape(1, LANES)
    row = pl.BlockSpec((1, LANES), lambda i: (0, 0))
    return pl.pallas_call(
        _rope_kernel,
        grid=(s // tm,),
        in_specs=[pl.BlockSpec((tm, 1), lambda i: (i, 0)), row, row],
        out_specs=[pl.BlockSpec((tm, LANES), lambda i: (i, 0))] * 2,
        out_shape=[jax.ShapeDtypeStruct((s, LANES), F32)] * 2,
        compiler_params=_params("parallel"),
        name="rope_tables",
    )(positions.reshape(s, 1), freq, sign)


def _rms_mod(x, g, sc, sh):
    ms = jnp.mean(x * x, axis=-1, keepdims=True)
    y = x * lax.rsqrt(ms + EPS) * g
    return y * (1.0 + sc) + sh


def _inproj_kernel(x_ref, g_ref, sc_ref, sh_ref, w_ref, qg_ref, kg_ref, cos_ref, sin_ref, bd_ref, cw_ref,
                   q_ref, k_ref, v_ref, conv_ref, ubuf_ref, *, tm):
    i = pl.program_id(0)
    h = _rms_mod(x_ref[...], g_ref[...], sc_ref[...], sh_ref[...])
    proj = _dot(h.astype(BF16), w_ref[...])
    cos = cos_ref[...]
    sin = sin_ref[...]
    lane = lax.broadcasted_iota(jnp.int32, (tm, LANES), 1)
    upper = (lane & (HEAD_DIM // 2)) != 0

    def head_norm_rope(z, gain_ref, out_ref, scale):
        for j in range(ATTN_WIDTH // 256):
            zj = z[:, 256 * j:256 * (j + 1)]
            ss = _dot((zj * zj).astype(BF16), bd_ref[...])
            zn = zj * lax.rsqrt(ss * (1.0 / HEAD_DIM) + EPS) * gain_ref[:, 256 * j:256 * (j + 1)]
            for t in range(2):
                zt = zn[:, LANES * t:LANES * (t + 1)]
                partner = jnp.where(upper, pltpu.roll(zt, HEAD_DIM // 2, 1),
                                    pltpu.roll(zt, LANES - HEAD_DIM // 2, 1))
                r = zt * cos + partner * sin
                c0 = 256 * j + LANES * t
                out_ref[:, c0:c0 + LANES] = (r * scale).astype(BF16)

    a = ATTN_WIDTH
    head_norm_rope(proj[:, 0:a], qg_ref, q_ref, HEAD_DIM ** -0.5)
    head_norm_rope(proj[:, a:2 * a], kg_ref, k_ref, 1.0)
    v_ref[...] = proj[:, 2 * a:3 * a].astype(BF16)

    c = CONV_WIDTH
    bg = proj[:, 3 * a:3 * a + c]
    cg = proj[:, 3 * a + c:3 * a + 2 * c]
    xv = proj[:, 3 * a + 2 * c:3 * a + 3 * c]
    u = cg * xv

    @pl.when(i == 0)
    def _():
        ubuf_ref[0:SUBLANES, :] = jnp.zeros((SUBLANES, c), F32)

    ubuf_ref[SUBLANES:SUBLANES + tm, :] = u
    u1 = ubuf_ref[SUBLANES - 1:SUBLANES - 1 + tm, :]
    u2 = ubuf_ref[SUBLANES - 2:SUBLANES - 2 + tm, :]
    y = u2 * cw_ref[0:1, :] + u1 * cw_ref[1:2, :] + u * cw_ref[2:3, :]
    conv_ref[...] = (bg * y).astype(BF16)
    ubuf_ref[0:SUBLANES, :] = ubuf_ref[tm:tm + SUBLANES, :]


def _inproj(x, g, sc, sh, w_in, qg, kg, cos_t, sin_t, bd, conv_w, tm):
    s = x.shape[0]
    row = lambda n: pl.BlockSpec((1, n), lambda i: (0, 0))
    tok = lambda n: pl.BlockSpec((tm, n), lambda i: (i, 0))
    return pl.pallas_call(
        functools.partial(_inproj_kernel, tm=tm),
        grid=(s // tm,),
        in_specs=[tok(D_MODEL), row(D_MODEL), row(D_MODEL), row(D_MODEL),
                  pl.BlockSpec((D_MODEL, IN_PROJ_WIDTH), lambda i: (0, 0)),
                  row(ATTN_WIDTH), row(ATTN_WIDTH), tok(LANES), tok(LANES),
                  pl.BlockSpec((256, 256), lambda i: (0, 0)),
                  pl.BlockSpec((3, CONV_WIDTH), lambda i: (0, 0))],
        out_specs=[tok(ATTN_WIDTH), tok(ATTN_WIDTH), tok(ATTN_WIDTH), tok(CONV_WIDTH)],
        out_shape=[jax.ShapeDtypeStruct((s, ATTN_WIDTH), BF16)] * 3
        + [jax.ShapeDtypeStruct((s, CONV_WIDTH), BF16)],
        scratch_shapes=[pltpu.VMEM((tm + SUBLANES, CONV_WIDTH), F32)],
        compiler_params=_params("arbitrary"),
        name="inproj",
    )(x, g, sc, sh, w_in, qg, kg, cos_t, sin_t, bd, conv_w)


def _attn_kernel(*refs, first, last):
    tq = BRANCH_KEYS
    if first:
        q_ref, kp_ref, kc_ref, vp_ref, vc_ref, acc_out, ml_out = refs
    elif last:
        q_ref, kp_ref, kc_ref, vp_ref, vc_ref, acc_in, ml_in, out_ref = refs
    else:
        q_ref, kp_ref, kc_ref, vp_ref, vc_ref, acc_in, ml_in, acc_out, ml_out = refs
    ib = pl.program_id(1)
    qi = lax.broadcasted_iota(jnp.int32, (tq, tq), 0)
    ki = lax.broadcasted_iota(jnp.int32, (tq, tq), 1)
    mask_c = ki <= qi
    mask_p = jnp.logical_and(ki >= qi, ib > 0)
    lane = ki
    ml_old = None if first else ml_in[...]
    heads = range(N_ATTN_HEADS)
    sls = [slice(HEAD_DIM * h, HEAD_DIM * (h + 1)) for h in heads]
    sp = [jnp.where(mask_p, _dot_nt(q_ref[:, sl], kp_ref[:, sl]), NEG_INF) for sl in sls]
    sc = [jnp.where(mask_c, _dot_nt(q_ref[:, sl], kc_ref[:, sl]), NEG_INF) for sl in sls]
    m_new = [jnp.max(jnp.maximum(a, b), axis=1, keepdims=True) for a, b in zip(sp, sc)]
    if not first:
        m_old = [jnp.max(jnp.where(lane == h, ml_old, NEG_INF), axis=1, keepdims=True) for h in heads]
        l_old = [jnp.sum(jnp.where(lane == 16 + h, ml_old, 0.0), axis=1, keepdims=True) for h in heads]
        m_new = [jnp.maximum(a, b) for a, b in zip(m_old, m_new)]
        alpha = [jnp.exp(a - b) for a, b in zip(m_old, m_new)]
    pp = [jnp.exp(s - m) for s, m in zip(sp, m_new)]
    pc = [jnp.exp(s - m) for s, m in zip(sc, m_new)]
    l_new = [jnp.sum(a + b, axis=1, keepdims=True) for a, b in zip(pp, pc)]
    acc = [_dot(a.astype(BF16), vp_ref[:, sl]) + _dot(b.astype(BF16), vc_ref[:, sl])
           for a, b, sl in zip(pp, pc, sls)]
    if not first:
        l_new = [a * lo + ln for a, lo, ln in zip(alpha, l_old, l_new)]
        acc = [a * acc_in[:, sl] + pv for a, sl, pv in zip(alpha, sls, acc)]
    if last:
        for sl, a, l in zip(sls, acc, l_new):
            out_ref[:, sl] = (a / l).astype(BF16)
    else:
        ml_new = jnp.zeros((tq, LANES), F32)
        for h, sl in zip(heads, sls):
            acc_out[:, sl] = acc[h]
            ml_new = jnp.where(lane == h, m_new[h], ml_new)
            ml_new = jnp.where(lane == 16 + h, l_new[h], ml_new)
        ml_out[...] = ml_new


def _attn_pass(q, k, v, state, dil, first, last):
    s = q.shape[0]
    n = s // dil
    tq = BRANCH_KEYS
    view = lambda a: a.reshape(n, dil * a.shape[1])
    cur = lambda w: pl.BlockSpec((tq, w), lambda r, ib: (ib, r))
    prev = lambda w: pl.BlockSpec((tq, w), lambda r, ib: (jnp.maximum(ib - 1, 0), r))
    aw = ATTN_WIDTH
    in_specs = [cur(aw), prev(aw), cur(aw), prev(aw), cur(aw)]
    args = [view(q), view(k), view(k), view(v), view(v)]
    if not first:
        in_specs += [cur(aw), cur(LANES)]
        args += [view(state[0]), view(state[1])]
    if last:
        out_specs = [cur(aw)]
        out_shape = [jax.ShapeDtypeStruct((n, dil * aw), BF16)]
    else:
        out_specs = [cur(aw), cur(LANES)]
        out_shape = [jax.ShapeDtypeStruct((n, dil * aw), F32), jax.ShapeDtypeStruct((n, dil * LANES), F32)]
    outs = pl.pallas_call(
        functools.partial(_attn_kernel, first=first, last=last),
        grid=(dil, n // tq),
        in_specs=in_specs, out_specs=out_specs, out_shape=out_shape,
        compiler_params=_params("parallel", "parallel"),
        name=f"attn_d{dil}",
    )(*args)
    return [o.reshape(s, -1) for o in outs]


def _attention(q, k, v):
    state = None
    for idx, dil in enumerate(DILATIONS):
        state = _attn_pass(q, k, v, state, dil, first=idx == 0, last=idx == len(DILATIONS) - 1)
    return state[0]


def _outproj_kernel(x_ref, attn_ref, conv_ref, woa_ref, woc_ref, gt_ref, g_ref, sc_ref, sh_ref, wq_ref,
                    keys_ref, x1_ref, h2_ref, st_ref):
    mix = _dot(attn_ref[...], woa_ref[...]) + _dot(conv_ref[...], woc_ref[...])
    x1 = x_ref[...] + gt_ref[...] * mix
    x1_ref[...] = x1
    h2 = _rms_mod(x1, g_ref[...], sc_ref[...], sh_ref[...]).astype(BF16)
    h2_ref[...] = h2
    qp = _dot(h2, wq_ref[...])
    for g in range(2 * PEER_HEADS):
        qg = qp[:, PEER_NKEYS * g:PEER_NKEYS * (g + 1)].astype(BF16)
        st_ref[PEER_NKEYS * g:PEER_NKEYS * (g + 1), :] = _dot_nt(keys_ref[g], qg)


def _outproj(x, attn, conv, wo_a, wo_c, gt, g, sc, sh, wq, keys, tm):
    s = x.shape[0]
    nq = wq.shape[1]
    row = lambda n: pl.BlockSpec((1, n), lambda i: (0, 0))
    tok = lambda n: pl.BlockSpec((tm, n), lambda i: (i, 0))
    full = lambda a: pl.BlockSpec(a.shape, lambda i: (0,) * a.ndim)
    return pl.pallas_call(
        _outproj_kernel,
        grid=(s // tm,),
        in_specs=[tok(D_MODEL), tok(ATTN_WIDTH), tok(CONV_WIDTH), full(wo_a), full(wo_c),
                  row(D_MODEL), row(D_MODEL), row(D_MODEL), row(D_MODEL), full(wq), full(keys)],
        out_specs=[tok(D_MODEL), tok(D_MODEL), pl.BlockSpec((nq, tm), lambda i: (0, i))],
        out_shape=[jax.ShapeDtypeStruct((s, D_MODEL), F32), jax.ShapeDtypeStruct((s, D_MODEL), BF16),
                   jax.ShapeDtypeStruct((nq, s), F32)],
        compiler_params=_params("parallel"),
        name="outproj",
    )(x, attn, conv, wo_a, wo_c, gt, g, sc, sh, wq, keys)


def _top16(ss, tb):
    n = ss[0].shape[0]
    row = lax.broadcasted_iota(jnp.int32, (n, tb), 0)
    row16 = lax.broadcasted_iota(jnp.int32, (PEER_TOPK, tb), 0)
    cur = list(ss)
    vals = [jnp.zeros((PEER_TOPK, tb), F32) for _ in ss]
    mark = lambda kk: -(F32_MAX - kk * F32_MAX_ULP)
    for kk in range(PEER_TOPK):
        m = [jnp.max(c, axis=0, keepdims=True) for c in cur]
        idx = [jnp.min(jnp.where(c == mm, row, n), axis=0, keepdims=True) for c, mm in zip(cur, m)]
        cur = [jnp.where(row == i, mark(kk), c) for i, c in zip(idx, cur)]
        vals = [jnp.where(row16 == kk, mm, v) for mm, v in zip(m, vals)]
    rank = [jnp.where(c <= mark(PEER_TOPK - 1), (c - mark(0)) * (1.0 / F32_MAX_ULP), float(PEER_TOPK))
            for c in cur]
    return vals, rank


def _topk_kernel(st_ref, rank2_ref, cnt_ref, e1_ref, e2_ref, *, tb):
    kk = PEER_TOPK
    s1 = st_ref[0:PEER_NKEYS, :]
    s2 = st_ref[PEER_NKEYS:2 * PEER_NKEYS, :]
    (v1, v2), (rank1, rank2) = _top16([s1, s2], tb)

    sub = lax.broadcasted_iota(jnp.int32, (SUBLANES, tb), 0)
    v2lo, v2hi = v2[0:SUBLANES, :], v2[SUBLANES:kk, :]
    groups = [(v1[0:1, :] + v2lo, sub, 0), (v1[0:1, :] + v2hi, sub + SUBLANES, 0),
              (v1[1:2, :] + v2lo, sub + kk, 1)]
    for a in range(2, SUBLANES):
        nb = kk // (a + 1)
        groups.append((jnp.where(sub < nb, v1[a:a + 1, :] + v2lo, NEG_INF), sub + kk * a, a))
    groups.append((v1[SUBLANES:kk, :] + v2[0:1, :], (sub + SUBLANES) * kk, None))
    cand = [g[0] for g in groups]
    flat = [g[1] for g in groups]
    cur = list(cand)
    big = kk * kk
    for _ in range(kk):
        m = functools.reduce(jnp.maximum, cur)
        m = jnp.max(m, axis=0, keepdims=True)
        hit = [jnp.where(c == m, f, big) for c, f in zip(cur, flat)]
        idx = jnp.min(functools.reduce(jnp.minimum, hit), axis=0, keepdims=True)
        cur = [jnp.where(f == idx, NEG_INF, c) for c, f in zip(cur, flat)]
    sel = [jnp.logical_and(c == NEG_INF, o != NEG_INF) for c, o in zip(cur, cand)]

    one = lambda mk: jnp.where(mk, 1.0, 0.0)
    counts = [jnp.sum(one(sel[0]) + one(sel[1]), axis=0, keepdims=True)]
    for gi in range(2, 2 + SUBLANES - 1):
        counts.append(jnp.sum(one(sel[gi]), axis=0, keepdims=True))
    tail = one(sel[-1])
    cnt = jnp.zeros((PEER_NKEYS, tb), F32)
    for a in range(kk):
        na = counts[a] if a < SUBLANES else tail[a - SUBLANES:a - SUBLANES + 1, :]
        cnt = jnp.where(rank1 == float(a), na, cnt)

    m1, m2 = v1[0:1, :], v2[0:1, :]
    mt = m1 + m2
    z = functools.reduce(lambda x, y: x + y,
                         [jnp.sum(jnp.where(sl, jnp.exp(o - mt), 0.0), axis=0, keepdims=True)
                          for sl, o in zip(sel, cand)])
    rank2_ref[...] = pltpu.bitcast(rank2.astype(BF16), jnp.uint32)
    cnt_ref[...] = cnt
    e1_ref[...] = jnp.exp(s1 - m1)
    e2_ref[...] = pltpu.bitcast((jnp.exp(s2 - m2) / z).astype(BF16), jnp.uint32)


def _topk(st, tb):
    s = st.shape[1]
    crow = PEER_NKEYS * SUBLANES // PACK
    plane = pl.BlockSpec((None, PEER_NKEYS, tb), lambda h, i: (h, 0, i))
    cplane = pl.BlockSpec((None, crow, tb), lambda h, i: (h, 0, i))
    shape = jax.ShapeDtypeStruct((PEER_HEADS, PEER_NKEYS, s), F32)
    cshape = jax.ShapeDtypeStruct((PEER_HEADS, crow, s), jnp.uint32)
    return pl.pallas_call(
        functools.partial(_topk_kernel, tb=tb),
        grid=(PEER_HEADS, s // tb),
        in_specs=[pl.BlockSpec((2 * PEER_NKEYS, tb), lambda h, i: (h, i))],
        out_specs=[cplane, plane, plane, cplane],
        out_shape=[cshape, shape, shape, cshape],
        compiler_params=_params("parallel", "parallel"),
        name="peer_topk",
    )(st)


PACK = 16
PEER_PIECES = 4


def _peer_kernel(h_ref, u_ref, vt_ref, rank2_ref, cnt_ref, e1_ref, e2_ref, x1_ref, gt_ref,
                 out_ref, acc_ref, p_ref, *, tm, ec):
    j = pl.program_id(1)
    rows = ec // PEER_NKEYS

    @pl.when(j == 0)
    def _():
        acc_ref[...] = jnp.zeros_like(acc_ref)

    crows = pl.ds(pl.multiple_of(j * rows, SUBLANES), rows)
    nt = tm // LANES
    tiles_per_piece = nt // PEER_PIECES
    er = ec // PEER_PIECES
    erw = er * SUBLANES // PACK
    zt = []
    for t in range(nt):
        if t % tiles_per_piece == 0:
            q = t // tiles_per_piece
            u_q = pltpu.bitcast(u_ref[erw * q:erw * (q + 1), :], BF16)
            zt.append(_dot_nt(u_q, h_ref[...]))
        tl = slice(LANES * t, LANES * (t + 1))
        cnt_rows = [cnt_ref[hh, crows, tl] for hh in range(PEER_HEADS)]
        e1_rows = [e1_ref[hh, crows, tl] for hh in range(PEER_HEADS)]
        for r in range(rows):
            w = [None] * (PEER_NKEYS // PACK)
            for hh in range(PEER_HEADS):
                cb = jnp.broadcast_to(cnt_rows[hh][r:r + 1, :], (PACK, LANES)).astype(BF16)
                eb = jnp.broadcast_to(e1_rows[hh][r:r + 1, :], (PACK, LANES)).astype(BF16)
                for g in range(PEER_NKEYS // PACK):
                    gs = slice(SUBLANES * g, SUBLANES * (g + 1))
                    rk = pltpu.bitcast(rank2_ref[hh, gs, tl], BF16)
                    e2 = pltpu.bitcast(e2_ref[hh, gs, tl], BF16)
                    term = jnp.where(rk < cb, e2, 0.0) * eb
                    w[g] = term if w[g] is None else w[g] + term
            p_ref[PEER_NKEYS * r:PEER_NKEYS * (r + 1), tl] = jnp.concatenate(w, axis=0)
    for q in range(PEER_PIECES):
        rs = slice(er * q, er * (q + 1))
        p_ref[rs, :] = p_ref[rs, :] * _gelu(zt[q].astype(BF16))
    acc_ref[...] += _dot(pltpu.bitcast(vt_ref[...], BF16), p_ref[...])

    @pl.when(j == pl.num_programs(1) - 1)
    def _():
        out_ref[...] = x1_ref[...] + gt_ref[...] * acc_ref[...].T


def _peer(h2, u, vt, planes, x1, gt, tm, ec):
    s = h2.shape[0]
    ne = vt.shape[1]
    assert ec == SUBLANES * PEER_NKEYS and (tm // LANES) % PEER_PIECES == 0, (ec, tm)
    plane = pl.BlockSpec((PEER_HEADS, PEER_NKEYS, tm), lambda i, j: (0, 0, i))
    cplane = pl.BlockSpec((PEER_HEADS, PEER_NKEYS * SUBLANES // PACK, tm), lambda i, j: (0, 0, i))
    tok = pl.BlockSpec((tm, D_MODEL), lambda i, j: (i, 0))
    return pl.pallas_call(
        functools.partial(_peer_kernel, tm=tm, ec=ec),
        grid=(s // tm, ne // ec),
        in_specs=[tok, pl.BlockSpec((ec * SUBLANES // PACK, D_MODEL), lambda i, j: (j, 0)),
                  pl.BlockSpec((D_MODEL * SUBLANES // PACK, ec), lambda i, j: (0, j)),
                  cplane, plane, plane, cplane, tok, pl.BlockSpec((1, D_MODEL), lambda i, j: (0, 0))],
        out_specs=tok,
        out_shape=jax.ShapeDtypeStruct((s, D_MODEL), F32),
        scratch_shapes=[pltpu.VMEM((D_MODEL, tm), F32), pltpu.VMEM((ec, tm), BF16)],
        compiler_params=_params("parallel", "arbitrary"),
        name="peer_dense",
    )(h2, u, vt, *planes, x1, gt)


def _words_kernel(x_ref, o_ref, *, transpose):
    x = x_ref[...]
    if transpose:
        x = x.T
    o_ref[...] = pltpu.bitcast(x.astype(BF16), jnp.uint32)


def _expert_table_words(tabs, layer, transpose):
    _, ne, d = tabs.shape
    blk = D_MODEL
    shrink = PACK // SUBLANES
    if transpose:
        out_shape, out_spec = (d // shrink, ne), pl.BlockSpec((d // shrink, blk), lambda i: (0, i))
    else:
        out_shape, out_spec = (ne // shrink, d), pl.BlockSpec((blk // shrink, d), lambda i: (i, 0))
    return pl.pallas_call(
        functools.partial(_words_kernel, transpose=transpose),
        grid=(ne // blk,),
        in_specs=[pl.BlockSpec((None, blk, d), lambda i: (layer, i, 0))],
        out_specs=out_spec,
        out_shape=jax.ShapeDtypeStruct(out_shape, jnp.uint32),
        compiler_params=_params("parallel"),
        name="expert_words_t" if transpose else "expert_words",
    )(tabs)


def _tile(s, want):
    t = min(want, s)
    assert s % t == 0, (s, t)
    return t


def _layer(x, mod, cos_t, sin_t, bd, norm_mix, norm_ffn, w_in, q_norm, k_norm, conv_w, w_out,
           peer_wq, peer_keys, peer_u_all, peer_v_all, layer):
    s = x.shape[0]
    sh1, sc1, gt1, sh2, sc2, gt2 = [m.reshape(1, D_MODEL) for m in jnp.split(mod, N_MOD)]
    tm = _tile(s, 512)
    qg = jnp.tile(q_norm, N_ATTN_HEADS).reshape(1, ATTN_WIDTH)
    kg = jnp.tile(k_norm, N_ATTN_HEADS).reshape(1, ATTN_WIDTH)
    q, k, v, conv = _inproj(x, norm_mix.reshape(1, D_MODEL), sc1, sh1, w_in.astype(BF16), qg, kg,
                            cos_t, sin_t, bd, conv_w, tm)
    attn = _attention(q, k, v)
    wo = w_out.astype(BF16)
    keys = peer_keys.reshape(2 * PEER_HEADS, PEER_NKEYS, -1).astype(BF16)
    x1, h2, st = _outproj(x, attn, conv, wo[:ATTN_WIDTH], wo[ATTN_WIDTH:], gt1,
                          norm_ffn.reshape(1, D_MODEL), sc2, sh2, peer_wq.astype(BF16), keys, tm)
    planes = _topk(st, _tile(s, 512))
    return _peer(h2, _expert_table_words(peer_u_all, layer, False), _expert_table_words(peer_v_all, layer, True),
                 planes, x1, gt2, _tile(s, 512), SUBLANES * PEER_NKEYS)


def kernel(x, c, positions, w_ada, b_ada, norm_mix, norm_ffn, w_in, q_norm, k_norm, conv_w, w_out,
           peer_wq, peer_keys, peer_u, peer_v):
    b, s, d = x.shape
    assert b == 1 and d == D_MODEL and s % (max(DILATIONS) * BRANCH_KEYS) == 0, x.shape
    depth = w_ada.shape[0]
    mod = _ada(c, w_ada, b_ada)
    cos_t, sin_t = _rope_tables(positions, _tile(s, 1024))
    blk = jnp.arange(256) // HEAD_DIM
    bd = (blk[:, None] == blk[None, :]).astype(BF16)
    y = x.reshape(s, d)
    for l in range(depth):
        y = _layer(y, mod[l], cos_t, sin_t, bd, norm_mix[l], norm_ffn[l], w_in[l], q_norm[l], k_norm[l],
                   conv_w[l], w_out[l], peer_wq[l], peer_keys[l], peer_u, peer_v, l)
    return y.reshape(b, s, d)
```

```python
import functools

import jax
import jax.numpy as jnp
from jax import lax
from jax.experimental import pallas as pl
from jax.experimental.pallas import tpu as pltpu

F32 = jnp.float32
BF16 = jnp.bfloat16

D_MODEL = 1024
HEAD_DIM = 64
N_ATTN_HEADS = 12
ATTN_WIDTH = N_ATTN_HEADS * HEAD_DIM
CONV_WIDTH = D_MODEL - ATTN_WIDTH
IN_PROJ_WIDTH = 3 * ATTN_WIDTH + 3 * CONV_WIDTH
BRANCH_KEYS = 128
DILATIONS = (16, 4, 1)
ROPE_THETA = 10000.0
EPS = 1e-6
PEER_HEADS = 8
PEER_NKEYS = 128
PEER_TOPK = 16
N_MOD = 6

LANES = 128
SUBLANES = 8
VMEM_LIMIT = 56 * 1024 * 1024

NEG_INF = float("-inf")
F32_MAX = float(jnp.finfo(jnp.float32).max)
F32_MAX_ULP = 2.0 ** 104


def _dot(a, b):
    return jnp.dot(a, b, preferred_element_type=F32)


def _dot_nt(a, b):
    return lax.dot_general(a, b, (((1,), (1,)), ((), ())), preferred_element_type=F32)


def _gelu(x):
    return 0.5 * x * (1.0 + lax.erf(x * (0.5 ** 0.5)))


def _params(*sem):
    return pltpu.CompilerParams(dimension_semantics=sem, vmem_limit_bytes=VMEM_LIMIT)


def _ada_kernel(c_ref, w_ref, b_ref, o_ref):
    o_ref[...] = jnp.dot(c_ref[...], w_ref[...], preferred_element_type=F32,
                         precision=lax.Precision.HIGHEST) + b_ref[...]


def _ada(c, w_ada, b_ada):
    depth = w_ada.shape[0]
    c8 = jnp.broadcast_to(c, (SUBLANES, D_MODEL))
    out = pl.pallas_call(
        _ada_kernel,
        grid=(depth, N_MOD),
        in_specs=[pl.BlockSpec((SUBLANES, D_MODEL), lambda l, j: (0, 0)),
                  pl.BlockSpec((None, D_MODEL, D_MODEL), lambda l, j: (l, 0, j)),
                  pl.BlockSpec((None, 1, D_MODEL), lambda l, j: (l, 0, j))],
        out_specs=pl.BlockSpec((None, SUBLANES, D_MODEL), lambda l, j: (l, 0, j)),
        out_shape=jax.ShapeDtypeStruct((depth, SUBLANES, N_MOD * D_MODEL), F32),
        compiler_params=_params("parallel", "parallel"),
        name="ada_mod",
    )(c8, w_ada, b_ada.reshape(depth, 1, N_MOD * D_MODEL))
    return out[:, 0, :]


def _rope_kernel(pos_ref, freq_ref, sign_ref, cos_ref, sin_ref):
    ang = pos_ref[...].astype(F32) * freq_ref[...]
    cos_ref[...] = jnp.cos(ang)
    sin_ref[...] = jnp.sin(ang) * sign_ref[...]


def _rope_tables(positions, tm):
    s = positions.shape[-1]
    half = HEAD_DIM // 2
    freq = ROPE_THETA ** (-jnp.arange(half, dtype=F32) / half)
    freq = jnp.tile(freq, LANES // half).reshape(1, LANES)
    sign = jnp.tile(jnp.concatenate([-jnp.ones((half,), F32), jnp.ones((half,), F32)]), LANES // HEAD_DIM)
    sign = sign.reshape(1, LANES)
    row = pl.BlockSpec((1, LANES), lambda i: (0, 0))
    return pl.pallas_call(
        _rope_kernel,
        grid=(s // tm,),
        in_specs=[pl.BlockSpec((tm, 1), lambda i: (i, 0)), row, row],
        out_specs=[pl.BlockSpec((tm, LANES), lambda i: (i, 0))] * 2,
        out_shape=[jax.ShapeDtypeStruct((s, LANES), F32)] * 2,
        compiler_params=_params("parallel"),
        name="rope_tables",
    )(positions.reshape(s, 1), freq, sign)


def _rms_mod(x, g, sc, sh):
    ms = jnp.mean(x * x, axis=-1, keepdims=True)
    y = x * lax.rsqrt(ms + EPS) * g
    return y * (1.0 + sc) + sh


def _inproj_kernel(x_ref, g_ref, sc_ref, sh_ref, w_ref, qg_ref, kg_ref, cos_ref, sin_ref, bd_ref, cw_ref,
                   q_ref, k_ref, v_ref, conv_ref, ubuf_ref, *, tm):
    i = pl.program_id(0)
    h = _rms_mod(x_ref[...], g_ref[...], sc_ref[...], sh_ref[...])
    proj = _dot(h.astype(BF16), w_ref[...])
    cos = cos_ref[...]
    sin = sin_ref[...]
    lane = lax.broadcasted_iota(jnp.int32, (tm, LANES), 1)
    upper = (lane & (HEAD_DIM // 2)) != 0

    def head_norm_rope(z, gain_ref, out_ref, scale):
        for j in range(ATTN_WIDTH // 256):
            zj = z[:, 256 * j:256 * (j + 1)]
            ss = _dot((zj * zj).astype(BF16), bd_ref[...])
            zn = zj * lax.rsqrt(ss * (1.0 / HEAD_DIM) + EPS) * gain_ref[:, 256 * j:256 * (j + 1)]
            for t in range(2):
                zt = zn[:, LANES * t:LANES * (t + 1)]
                partner = jnp.where(upper, pltpu.roll(zt, HEAD_DIM // 2, 1),
                                    pltpu.roll(zt, LANES - HEAD_DIM // 2, 1))
                r = zt * cos + partner * sin
                c0 = 256 * j + LANES * t
                out_ref[:, c0:c0 + LANES] = (r * scale).astype(BF16)

    a = ATTN_WIDTH
    head_norm_rope(proj[:, 0:a], qg_ref, q_ref, HEAD_DIM ** -0.5)
    head_norm_rope(proj[:, a:2 * a], kg_ref, k_ref, 1.0)
    v_ref[...] = proj[:, 2 * a:3 * a].astype(BF16)

    c = CONV_WIDTH
    bg = proj[:, 3 * a:3 * a + c]
    cg = proj[:, 3 * a + c:3 * a + 2 * c]
    xv = proj[:, 3 * a + 2 * c:3 * a + 3 * c]
    u = cg * xv

    @pl.when(i == 0)
    def _():
        ubuf_ref[0:SUBLANES, :] = jnp.zeros((SUBLANES, c), F32)

    ubuf_ref[SUBLANES:SUBLANES + tm, :] = u
    u1 = ubuf_ref[SUBLANES - 1:SUBLANES - 1 + tm, :]
    u2 = ubuf_ref[SUBLANES - 2:SUBLANES - 2 + tm, :]
    y = u2 * cw_ref[0:1, :] + u1 * cw_ref[1:2, :] + u * cw_ref[2:3, :]
    conv_ref[...] = (bg * y).astype(BF16)
    ubuf_ref[0:SUBLANES, :] = ubuf_ref[tm:tm + SUBLANES, :]


def _inproj(x, g, sc, sh, w_in, qg, kg, cos_t, sin_t, bd, conv_w, tm):
    s = x.shape[0]
    row = lambda n: pl.BlockSpec((1, n), lambda i: (0, 0))
    tok = lambda n: pl.BlockSpec((tm, n), lambda i: (i, 0))
    return pl.pallas_call(
        functools.partial(_inproj_kernel, tm=tm),
        grid=(s // tm,),
        in_specs=[tok(D_MODEL), row(D_MODEL), row(D_MODEL), row(D_MODEL),
                  pl.BlockSpec((D_MODEL, IN_PROJ_WIDTH), lambda i: (0, 0)),
                  row(ATTN_WIDTH), row(ATTN_WIDTH), tok(LANES), tok(LANES),
                  pl.BlockSpec((256, 256), lambda i: (0, 0)),
                  pl.BlockSpec((3, CONV_WIDTH), lambda i: (0, 0))],
        out_specs=[tok(ATTN_WIDTH), tok(ATTN_WIDTH), tok(ATTN_WIDTH), tok(CONV_WIDTH)],
        out_shape=[jax.ShapeDtypeStruct((s, ATTN_WIDTH), BF16)] * 3
        + [jax.ShapeDtypeStruct((s, CONV_WIDTH), BF16)],
        scratch_shapes=[pltpu.VMEM((tm + SUBLANES, CONV_WIDTH), F32)],
        compiler_params=_params("arbitrary"),
        name="inproj",
    )(x, g, sc, sh, w_in, qg, kg, cos_t, sin_t, bd, conv_w)


def _attn_kernel(*refs, first, last):
    tq = BRANCH_KEYS
    if first:
        q_ref, kp_ref, kc_ref, vp_ref, vc_ref, acc_out, ml_out = refs
    elif last:
        q_ref, kp_ref, kc_ref, vp_ref, vc_ref, acc_in, ml_in, out_ref = refs
    else:
        q_ref, kp_ref, kc_ref, vp_ref, vc_ref, acc_in, ml_in, acc_out, ml_out = refs
    ib = pl.program_id(1)
    qi = lax.broadcasted_iota(jnp.int32, (tq, tq), 0)
    ki = lax.broadcasted_iota(jnp.int32, (tq, tq), 1)
    mask_c = ki <= qi
    mask_p = jnp.logical_and(ki >= qi, ib > 0)
    lane = ki
    ml_old = None if first else ml_in[...]
    heads = range(N_ATTN_HEADS)
    sls = [slice(HEAD_DIM * h, HEAD_DIM * (h + 1)) for h in heads]
    sp = [jnp.where(mask_p, _dot_nt(q_ref[:, sl], kp_ref[:, sl]), NEG_INF) for sl in sls]
    sc = [jnp.where(mask_c, _dot_nt(q_ref[:, sl], kc_ref[:, sl]), NEG_INF) for sl in sls]
    m_new = [jnp.max(jnp.maximum(a, b), axis=1, keepdims=True) for a, b in zip(sp, sc)]
    if not first:
        m_old = [jnp.max(jnp.where(lane == h, ml_old, NEG_INF), axis=1, keepdims=True) for h in heads]
        l_old = [jnp.sum(jnp.where(lane == 16 + h, ml_old, 0.0), axis=1, keepdims=True) for h in heads]
        m_new = [jnp.maximum(a, b) for a, b in zip(m_old, m_new)]
        alpha = [jnp.exp(a - b) for a, b in zip(m_old, m_new)]
    pp = [jnp.exp(s - m) for s, m in zip(sp, m_new)]
    pc = [jnp.exp(s - m) for s, m in zip(sc, m_new)]
    l_new = [jnp.sum(a + b, axis=1, keepdims=True) for a, b in zip(pp, pc)]
    acc = [_dot(a.astype(BF16), vp_ref[:, sl]) + _dot(b.astype(BF16), vc_ref[:, sl])
           for a, b, sl in zip(pp, pc, sls)]
    if not first:
        l_new = [a * lo + ln for a, lo, ln in zip(alpha, l_old, l_new)]
        acc = [a * acc_in[:, sl] + pv for a, sl, pv in zip(alpha, sls, acc)]
    if last:
        for sl, a, l in zip(sls, acc, l_new):
            out_ref[:, sl] = (a / l).astype(BF16)
    else:
        ml_new = jnp.zeros((tq, LANES), F32)
        for h, sl in zip(heads, sls):
            acc_out[:, sl] = acc[h]
            ml_new = jnp.where(lane == h, m_new[h], ml_new)
            ml_new = jnp.where(lane == 16 + h, l_new[h], ml_new)
        ml_out[...] = ml_new


def _attn_pass(q, k, v, state, dil, first, last):
    s = q.shape[0]
    n = s // dil
    tq = BRANCH_KEYS
    view = lambda a: a.reshape(n, dil * a.shape[1])
    cur = lambda w: pl.BlockSpec((tq, w), lambda r, ib: (ib, r))
    prev = lambda w: pl.BlockSpec((tq, w), lambda r, ib: (jnp.maximum(ib - 1, 0), r))
    aw = ATTN_WIDTH
    in_specs = [cur(aw), prev(aw), cur(aw), prev(aw), cur(aw)]
    args = [view(q), view(k), view(k), view(v), view(v)]
    if not first:
        in_specs += [cur(aw), cur(LANES)]
        args += [view(state[0]), view(state[1])]
    if last:
        out_specs = [cur(aw)]
        out_shape = [jax.ShapeDtypeStruct((n, dil * aw), BF16)]
    else:
        out_specs = [cur(aw), cur(LANES)]
        out_shape = [jax.ShapeDtypeStruct((n, dil * aw), F32), jax.ShapeDtypeStruct((n, dil * LANES), F32)]
    outs = pl.pallas_call(
        functools.partial(_attn_kernel, first=first, last=last),
        grid=(dil, n // tq),
        in_specs=in_specs, out_specs=out_specs, out_shape=out_shape,
        compiler_params=_params("parallel", "parallel"),
        name=f"attn_d{dil}",
    )(*args)
    return [o.reshape(s, -1) for o in outs]


def _attention(q, k, v):
    state = None
    for idx, dil in enumerate(DILATIONS):
        state = _attn_pass(q, k, v, state, dil, first=idx == 0, last=idx == len(DILATIONS) - 1)
    return state[0]


def _outproj_kernel(x_ref, attn_ref, conv_ref, woa_ref, woc_ref, gt_ref, g_ref, sc_ref, sh_ref, wq_ref,
                    keys_ref, x1_ref, h2_ref, st_ref):
    mix = _dot(attn_ref[...], woa_ref[...]) + _dot(conv_ref[...], woc_ref[...])
    x1 = x_ref[...] + gt_ref[...] * mix
    x1_ref[...] = x1
    h2 = _rms_mod(x1, g_ref[...], sc_ref[...], sh_ref[...]).astype(BF16)
    h2_ref[...] = h2
    qp = _dot(h2, wq_ref[...])
    for g in range(2 * PEER_HEADS):
        qg = qp[:, PEER_NKEYS * g:PEER_NKEYS * (g + 1)].astype(BF16)
        st_ref[PEER_NKEYS * g:PEER_NKEYS * (g + 1), :] = _dot_nt(keys_ref[g], qg)


def _outproj(x, attn, conv, wo_a, wo_c, gt, g, sc, sh, wq, keys, tm):
    s = x.shape[0]
    nq = wq.shape[1]
    row = lambda n: pl.BlockSpec((1, n), lambda i: (0, 0))
    tok = lambda n: pl.BlockSpec((tm, n), lambda i: (i, 0))
    full = lambda a: pl.BlockSpec(a.shape, lambda i: (0,) * a.ndim)
    return pl.pallas_call(
        _outproj_kernel,
        grid=(s // tm,),
        in_specs=[tok(D_MODEL), tok(ATTN_WIDTH), tok(CONV_WIDTH), full(wo_a), full(wo_c),
                  row(D_MODEL), row(D_MODEL), row(D_MODEL), row(D_MODEL), full(wq), full(keys)],
        out_specs=[tok(D_MODEL), tok(D_MODEL), pl.BlockSpec((nq, tm), lambda i: (0, i))],
        out_shape=[jax.ShapeDtypeStruct((s, D_MODEL), F32), jax.ShapeDtypeStruct((s, D_MODEL), BF16),
                   jax.ShapeDtypeStruct((nq, s), F32)],
        compiler_params=_params("parallel"),
        name="outproj",
    )(x, attn, conv, wo_a, wo_c, gt, g, sc, sh, wq, keys)


def _top16(ss, tb):
    n = ss[0].shape[0]
    row = lax.broadcasted_iota(jnp.int32, (n, tb), 0)
    row16 = lax.broadcasted_iota(jnp.int32, (PEER_TOPK, tb), 0)
    cur = list(ss)
    vals = [jnp.zeros((PEER_TOPK, tb), F32) for _ in ss]
    mark = lambda kk: -(F32_MAX - kk * F32_MAX_ULP)
    for kk in range(PEER_TOPK):
        m = [jnp.max(c, axis=0, keepdims=True) for c in cur]
        idx = [jnp.min(jnp.where(c == mm, row, n), axis=0, keepdims=True) for c, mm in zip(cur, m)]
        cur = [jnp.where(row == i, mark(kk), c) for i, c in zip(idx, cur)]
        vals = [jnp.where(row16 == kk, mm, v) for mm, v in zip(m, vals)]
    rank = [jnp.where(c <= mark(PEER_TOPK - 1), (c - mark(0)) * (1.0 / F32_MAX_ULP), float(PEER_TOPK))
            for c in cur]
    return vals, rank


def _topk_kernel(st_ref, rank2_ref, cnt_ref, e1_ref, e2_ref, *, tb):
    kk = PEER_TOPK
    s1 = st_ref[0:PEER_NKEYS, :]
    s2 = st_ref[PEER_NKEYS:2 * PEER_NKEYS, :]
    (v1, v2), (rank1, rank2) = _top16([s1, s2], tb)

    sub = lax.broadcasted_iota(jnp.int32, (SUBLANES, tb), 0)
    v2lo, v2hi = v2[0:SUBLANES, :], v2[SUBLANES:kk, :]
    groups = [(v1[0:1, :] + v2lo, sub, 0), (v1[0:1, :] + v2hi, sub + SUBLANES, 0),
              (v1[1:2, :] + v2lo, sub + kk, 1)]
    for a in range(2, SUBLANES):
        nb = kk // (a + 1)
        groups.append((jnp.where(sub < nb, v1[a:a + 1, :] + v2lo, NEG_INF), sub + kk * a, a))
    groups.append((v1[SUBLANES:kk, :] + v2[0:1, :], (sub + SUBLANES) * kk, None))
    cand = [g[0] for g in groups]
    flat = [g[1] for g in groups]
    cur = list(cand)
    big = kk * kk
    for _ in range(kk):
        m = functools.reduce(jnp.maximum, cur)
        m = jnp.max(m, axis=0, keepdims=True)
        hit = [jnp.where(c == m, f, big) for c, f in zip(cur, flat)]
        idx = jnp.min(functools.reduce(jnp.minimum, hit), axis=0, keepdims=True)
        cur = [jnp.where(f == idx, NEG_INF, c) for c, f in zip(cur, flat)]
    sel = [jnp.logical_and(c == NEG_INF, o != NEG_INF) for c, o in zip(cur, cand)]

    one = lambda mk: jnp.where(mk, 1.0, 0.0)
    counts = [jnp.sum(one(sel[0]) + one(sel[1]), axis=0, keepdims=True)]
    for gi in range(2, 2 + SUBLANES - 1):
        counts.append(jnp.sum(one(sel[gi]), axis=0, keepdims=True))
    tail = one(sel[-1])
    cnt = jnp.zeros((PEER_NKEYS, tb), F32)
    for a in range(kk):
        na = counts[a] if a < SUBLANES else tail[a - SUBLANES:a - SUBLANES + 1, :]
        cnt = jnp.where(rank1 == float(a), na, cnt)

    m1, m2 = v1[0:1, :], v2[0:1, :]
    mt = m1 + m2
    z = functools.reduce(lambda x, y: x + y,
                         [jnp.sum(jnp.where(sl, jnp.exp(o - mt), 0.0), axis=0, keepdims=True)
                          for sl, o in zip(sel, cand)])
    rank2_ref[...] = pltpu.bitcast(rank2.astype(BF16), jnp.uint32)
    cnt_ref[...] = cnt
    e1_ref[...] = jnp.exp(s1 - m1)
    e2_ref[...] = pltpu.bitcast((jnp.exp(s2 - m2) / z).astype(BF16), jnp.uint32)


def _topk(st, tb):
    s = st.shape[1]
    crow = PEER_NKEYS * SUBLANES // PACK
    plane = pl.BlockSpec((None, PEER_NKEYS, tb), lambda h, i: (h, 0, i))
    cplane = pl.BlockSpec((None, crow, tb), lambda h, i: (h, 0, i))
    shape = jax.ShapeDtypeStruct((PEER_HEADS, PEER_NKEYS, s), F32)
    cshape = jax.ShapeDtypeStruct((PEER_HEADS, crow, s), jnp.uint32)
    return pl.pallas_call(
        functools.partial(_topk_kernel, tb=tb),
        grid=(PEER_HEADS, s // tb),
        in_specs=[pl.BlockSpec((2 * PEER_NKEYS, tb), lambda h, i: (h, i))],
        out_specs=[cplane, plane, plane, cplane],
        out_shape=[cshape, shape, shape, cshape],
        compiler_params=_params("parallel", "parallel"),
        name="peer_topk",
    )(st)


PACK = 16
PEER_PIECES = 4


def _peer_kernel(h_ref, u_ref, vt_ref, rank2_ref, cnt_ref, e1_ref, e2_ref, x1_ref, gt_ref,
                 out_ref, acc_ref, p0_ref, p1_ref, *, tm, ec):
    j = pl.program_id(1)
    last = pl.num_programs(1) - 1
    rows = ec // PEER_NKEYS
    nt = tm // LANES
    tiles_per_piece = nt // PEER_PIECES
    er = ec // PEER_PIECES
    erw = er * SUBLANES // PACK
    dr = D_MODEL // PEER_PIECES
    drw = dr * SUBLANES // PACK

    @pl.when(j == 0)
    def _():
        acc_ref[...] = jnp.zeros_like(acc_ref)
        p1_ref[...] = jnp.zeros_like(p1_ref)

    def second_matmul_piece(q, p_prev):
        ds_ = slice(dr * q, dr * (q + 1))
        v_q = pltpu.bitcast(vt_ref[drw * q:drw * (q + 1), :], BF16)
        acc_ref[ds_, :] += _dot(v_q, p_prev[...])

    def full_step(p_cur, p_prev):
        crows = pl.ds(pl.multiple_of(j * rows, SUBLANES), rows)
        zt = []
        for t in range(nt):
            if t % tiles_per_piece == 0:
                q = t // tiles_per_piece
                u_q = pltpu.bitcast(u_ref[erw * q:erw * (q + 1), :], BF16)
                zt.append(_dot_nt(u_q, h_ref[...]))
                second_matmul_piece(q, p_prev)
            tl = slice(LANES * t, LANES * (t + 1))
            cnt_rows = [cnt_ref[hh, crows, tl] for hh in range(PEER_HEADS)]
            e1_rows = [e1_ref[hh, crows, tl] for hh in range(PEER_HEADS)]
            for r in range(rows):
                w = [None] * (PEER_NKEYS // PACK)
                for hh in range(PEER_HEADS):
                    cb = jnp.broadcast_to(cnt_rows[hh][r:r + 1, :], (PACK, LANES)).astype(BF16)
                    eb = jnp.broadcast_to(e1_rows[hh][r:r + 1, :], (PACK, LANES)).astype(BF16)
                    for g in range(PEER_NKEYS // PACK):
                        gs = slice(SUBLANES * g, SUBLANES * (g + 1))
                        rk = pltpu.bitcast(rank2_ref[hh, gs, tl], BF16)
                        e2 = pltpu.bitcast(e2_ref[hh, gs, tl], BF16)
                        term = jnp.where(rk < cb, e2, 0.0) * eb
                        w[g] = term if w[g] is None else w[g] + term
                p_cur[PEER_NKEYS * r:PEER_NKEYS * (r + 1), tl] = jnp.concatenate(w, axis=0)
        for q in range(PEER_PIECES):
            rs = slice(er * q, er * (q + 1))
            p_cur[rs, :] = p_cur[rs, :] * _gelu(zt[q].astype(BF16))

    even = (j & 1) == 0

    @pl.when(jnp.logical_and(j < last, even))
    def _():
        full_step(p0_ref, p1_ref)

    @pl.when(jnp.logical_and(j < last, jnp.logical_not(even)))
    def _():
        full_step(p1_ref, p0_ref)

    @pl.when(j == last)
    def _():
        for q in range(PEER_PIECES):
            second_matmul_piece(q, p1_ref)
        out_ref[...] = x1_ref[...] + gt_ref[...] * acc_ref[...].T


def _peer(h2, u, vt, planes, x1, gt, tm, ec):
    s = h2.shape[0]
    ne = vt.shape[1]
    nj = ne // ec
    assert ec == SUBLANES * PEER_NKEYS and (tm // LANES) % PEER_PIECES == 0 and nj % 2 == 0, (ec, tm, nj)
    plane = pl.BlockSpec((PEER_HEADS, PEER_NKEYS, tm), lambda i, j: (0, 0, i))
    cplane = pl.BlockSpec((PEER_HEADS, PEER_NKEYS * SUBLANES // PACK, tm), lambda i, j: (0, 0, i))
    tok = pl.BlockSpec((tm, D_MODEL), lambda i, j: (i, 0))
    return pl.pallas_call(
        functools.partial(_peer_kernel, tm=tm, ec=ec),
        grid=(s // tm, nj + 1),
        in_specs=[tok,
                  pl.BlockSpec((ec * SUBLANES // PACK, D_MODEL), lambda i, j: (jnp.minimum(j, nj - 1), 0)),
                  pl.BlockSpec((D_MODEL * SUBLANES // PACK, ec), lambda i, j: (0, jnp.maximum(j - 1, 0))),
                  cplane, plane, plane, cplane, tok, pl.BlockSpec((1, D_MODEL), lambda i, j: (0, 0))],
        out_specs=tok,
        out_shape=jax.ShapeDtypeStruct((s, D_MODEL), F32),
        scratch_shapes=[pltpu.VMEM((D_MODEL, tm), F32), pltpu.VMEM((ec, tm), BF16), pltpu.VMEM((ec, tm), BF16)],
        compiler_params=_params("parallel", "arbitrary"),
        name="peer_dense",
    )(h2, u, vt, *planes, x1, gt)


def _words_kernel(x_ref, o_ref, *, transpose):
    x = x_ref[...]
    if transpose:
        x = x.T
    o_ref[...] = pltpu.bitcast(x.astype(BF16), jnp.uint32)


def _expert_table_words(tabs, layer, transpose):
    _, ne, d = tabs.shape
    blk = D_MODEL
    shrink = PACK // SUBLANES
    if transpose:
        out_shape, out_spec = (d // shrink, ne), pl.BlockSpec((d // shrink, blk), lambda i: (0, i))
    else:
        out_shape, out_spec = (ne // shrink, d), pl.BlockSpec((blk // shrink, d), lambda i: (i, 0))
    return pl.pallas_call(
        functools.partial(_words_kernel, transpose=transpose),
        grid=(ne // blk,),
        in_specs=[pl.BlockSpec((None, blk, d), lambda i: (layer, i, 0))],
        out_specs=out_spec,
        out_shape=jax.ShapeDtypeStruct(out_shape, jnp.uint32),
        compiler_params=_params("parallel"),
        name="expert_words_t" if transpose else "expert_words",
    )(tabs)


def _tile(s, want):
    t = min(want, s)
    assert s % t == 0, (s, t)
    return t


def _layer(x, mod, cos_t, sin_t, bd, norm_mix, norm_ffn, w_in, q_norm, k_norm, conv_w, w_out,
           peer_wq, peer_keys, peer_u_all, peer_v_all, layer):
    s = x.shape[0]
    sh1, sc1, gt1, sh2, sc2, gt2 = [m.reshape(1, D_MODEL) for m in jnp.split(mod, N_MOD)]
    tm = _tile(s, 512)
    qg = jnp.tile(q_norm, N_ATTN_HEADS).reshape(1, ATTN_WIDTH)
    kg = jnp.tile(k_norm, N_ATTN_HEADS).reshape(1, ATTN_WIDTH)
    q, k, v, conv = _inproj(x, norm_mix.reshape(1, D_MODEL), sc1, sh1, w_in.astype(BF16), qg, kg,
                            cos_t, sin_t, bd, conv_w, tm)
    attn = _attention(q, k, v)
    wo = w_out.astype(BF16)
    keys = peer_keys.reshape(2 * PEER_HEADS, PEER_NKEYS, -1).astype(BF16)
    x1, h2, st = _outproj(x, attn, conv, wo[:ATTN_WIDTH], wo[ATTN_WIDTH:], gt1,
                          norm_ffn.reshape(1, D_MODEL), sc2, sh2, peer_wq.astype(BF16), keys, tm)
    planes = _topk(st, _tile(s, 512))
    return _peer(h2, _expert_table_words(peer_u_all, layer, False), _expert_table_words(peer_v_all, layer, True),
                 planes, x1, gt2, _tile(s, 512), SUBLANES * PEER_NKEYS)


def kernel(x, c, positions, w_ada, b_ada, norm_mix, norm_ffn, w_in, q_norm, k_norm, conv_w, w_out,
           peer_wq, peer_keys, peer_u, peer_v):
    b, s, d = x.shape
    assert b == 1 and d == D_MODEL and s % (max(DILATIONS) * BRANCH_KEYS) == 0, x.shape
    depth = w_ada.shape[0]
    mod = _ada(c, w_ada, b_ada)
    cos_t, sin_t = _rope_tables(positions, _tile(s, 1024))
    blk = jnp.arange(256) // HEAD_DIM
    bd = (blk[:, None] == blk[None, :]).astype(BF16)
    y = x.reshape(s, d)
    for l in range(depth):
        y = _layer(y, mod[l], cos_t, sin_t, bd, norm_mix[l], norm_ffn[l], w_in[l], q_norm[l], k_norm[l],
                   conv_w[l], w_out[l], peer_wq[l], peer_keys[l], peer_u, peer_v, l)
    return y.reshape(b, s, d)
```

```python
import functools

import jax
import jax.numpy as jnp
from jax import lax
from jax.experimental import pallas as pl
from jax.experimental.pallas import tpu as pltpu

F32 = jnp.float32
BF16 = jnp.bfloat16

D_MODEL = 1024
HEAD_DIM = 64
N_ATTN_HEADS = 12
ATTN_WIDTH = N_ATTN_HEADS * HEAD_DIM
CONV_WIDTH = D_MODEL - ATTN_WIDTH
IN_PROJ_WIDTH = 3 * ATTN_WIDTH + 3 * CONV_WIDTH
BRANCH_KEYS = 128
DILATIONS = (16, 4, 1)
ROPE_THETA = 10000.0
EPS = 1e-6
PEER_HEADS = 8
PEER_NKEYS = 128
PEER_TOPK = 16
N_MOD = 6

LANES = 128
SUBLANES = 8
VMEM_LIMIT = 56 * 1024 * 1024

NEG_INF = float("-inf")
F32_MAX = float(jnp.finfo(jnp.float32).max)
F32_MAX_ULP = 2.0 ** 104


def _dot(a, b):
    return jnp.dot(a, b, preferred_element_type=F32)


def _dot_nt(a, b):
    return lax.dot_general(a, b, (((1,), (1,)), ((), ())), preferred_element_type=F32)


def _gelu(x):
    return 0.5 * x * (1.0 + lax.erf(x * (0.5 ** 0.5)))


def _params(*sem):
    return pltpu.CompilerParams(dimension_semantics=sem, vmem_limit_bytes=VMEM_LIMIT)


def _ada_kernel(c_ref, w_ref, b_ref, o_ref):
    o_ref[...] = jnp.dot(c_ref[...], w_ref[...], preferred_element_type=F32,
                         precision=lax.Precision.HIGHEST) + b_ref[...]


def _ada(c, w_ada, b_ada):
    depth = w_ada.shape[0]
    c8 = jnp.broadcast_to(c, (SUBLANES, D_MODEL))
    out = pl.pallas_call(
        _ada_kernel,
        grid=(depth, N_MOD),
        in_specs=[pl.BlockSpec((SUBLANES, D_MODEL), lambda l, j: (0, 0)),
                  pl.BlockSpec((None, D_MODEL, D_MODEL), lambda l, j: (l, 0, j)),
                  pl.BlockSpec((None, 1, D_MODEL), lambda l, j: (l, 0, j))],
        out_specs=pl.BlockSpec((None, SUBLANES, D_MODEL), lambda l, j: (l, 0, j)),
        out_shape=jax.ShapeDtypeStruct((depth, SUBLANES, N_MOD * D_MODEL), F32),
        compiler_params=_params("parallel", "parallel"),
        name="ada_mod",
    )(c8, w_ada, b_ada.reshape(depth, 1, N_MOD * D_MODEL))
    return out[:, 0, :]


def _rope_kernel(pos_ref, freq_ref, sign_ref, cos_ref, sin_ref):
    ang = pos_ref[...].astype(F32) * freq_ref[...]
    cos_ref[...] = jnp.cos(ang)
    sin_ref[...] = jnp.sin(ang) * sign_ref[...]


def _rope_tables(positions, tm):
    s = positions.shape[-1]
    half = HEAD_DIM // 2
    freq = ROPE_THETA ** (-jnp.arange(half, dtype=F32) / half)
    freq = jnp.tile(freq, LANES // half).reshape(1, LANES)
    sign = jnp.tile(jnp.concatenate([-jnp.ones((half,), F32), jnp.ones((half,), F32)]), LANES // HEAD_DIM)
    sign = sign.reshape(1, LANES)
    row = pl.BlockSpec((1, LANES), lambda i: (0, 0))
    return pl.pallas_call(
        _rope_kernel,
        grid=(s // tm,),
        in_specs=[pl.BlockSpec((tm, 1), lambda i: (i, 0)), row, row],
        out_specs=[pl.BlockSpec((tm, LANES), lambda i: (i, 0))] * 2,
        out_shape=[jax.ShapeDtypeStruct((s, LANES), F32)] * 2,
        compiler_params=_params("parallel"),
        name="rope_tables",
    )(positions.reshape(s, 1), freq, sign)


def _rms_mod(x, g, sc, sh):
    ms = jnp.mean(x * x, axis=-1, keepdims=True)
    y = x * lax.rsqrt(ms + EPS) * g
    return y * (1.0 + sc) + sh


def _inproj_kernel(x_ref, g_ref, sc_ref, sh_ref, w_ref, qg_ref, kg_ref, cos_ref, sin_ref, bd_ref, cw_ref,
                   q_ref, k_ref, v_ref, conv_ref, ubuf_ref, *, tm):
    i = pl.program_id(0)
    h = _rms_mod(x_ref[...], g_ref[...], sc_ref[...], sh_ref[...])
    proj = _dot(h.astype(BF16), w_ref[...])
    cos = cos_ref[...]
    sin = sin_ref[...]
    lane = lax.broadcasted_iota(jnp.int32, (tm, LANES), 1)
    upper = (lane & (HEAD_DIM // 2)) != 0

    def head_norm_rope(z, gain_ref, out_ref, scale):
        for j in range(ATTN_WIDTH // 256):
            zj = z[:, 256 * j:256 * (j + 1)]
            ss = _dot((zj * zj).astype(BF16), bd_ref[...])
            zn = zj * lax.rsqrt(ss * (1.0 / HEAD_DIM) + EPS) * gain_ref[:, 256 * j:256 * (j + 1)]
            for t in range(2):
                zt = zn[:, LANES * t:LANES * (t + 1)]
                partner = jnp.where(upper, pltpu.roll(zt, HEAD_DIM // 2, 1),
                                    pltpu.roll(zt, LANES - HEAD_DIM // 2, 1))
                r = zt * cos + partner * sin
                c0 = 256 * j + LANES * t
                out_ref[:, c0:c0 + LANES] = (r * scale).astype(BF16)

    a = ATTN_WIDTH
    head_norm_rope(proj[:, 0:a], qg_ref, q_ref, HEAD_DIM ** -0.5)
    head_norm_rope(proj[:, a:2 * a], kg_ref, k_ref, 1.0)
    v_ref[...] = proj[:, 2 * a:3 * a].astype(BF16)

    c = CONV_WIDTH
    bg = proj[:, 3 * a:3 * a + c]
    cg = proj[:, 3 * a + c:3 * a + 2 * c]
    xv = proj[:, 3 * a + 2 * c:3 * a + 3 * c]
    u = cg * xv

    @pl.when(i == 0)
    def _():
        ubuf_ref[0:SUBLANES, :] = jnp.zeros((SUBLANES, c), F32)

    ubuf_ref[SUBLANES:SUBLANES + tm, :] = u
    u1 = ubuf_ref[SUBLANES - 1:SUBLANES - 1 + tm, :]
    u2 = ubuf_ref[SUBLANES - 2:SUBLANES - 2 + tm, :]
    y = u2 * cw_ref[0:1, :] + u1 * cw_ref[1:2, :] + u * cw_ref[2:3, :]
    conv_ref[...] = (bg * y).astype(BF16)
    ubuf_ref[0:SUBLANES, :] = ubuf_ref[tm:tm + SUBLANES, :]


def _inproj(x, g, sc, sh, w_in, qg, kg, cos_t, sin_t, bd, conv_w, tm):
    s = x.shape[0]
    row = lambda n: pl.BlockSpec((1, n), lambda i: (0, 0))
    tok = lambda n: pl.BlockSpec((tm, n), lambda i: (i, 0))
    return pl.pallas_call(
        functools.partial(_inproj_kernel, tm=tm),
        grid=(s // tm,),
        in_specs=[tok(D_MODEL), row(D_MODEL), row(D_MODEL), row(D_MODEL),
                  pl.BlockSpec((D_MODEL, IN_PROJ_WIDTH), lambda i: (0, 0)),
                  row(ATTN_WIDTH), row(ATTN_WIDTH), tok(LANES), tok(LANES),
                  pl.BlockSpec((256, 256), lambda i: (0, 0)),
                  pl.BlockSpec((3, CONV_WIDTH), lambda i: (0, 0))],
        out_specs=[tok(ATTN_WIDTH), tok(ATTN_WIDTH), tok(ATTN_WIDTH), tok(CONV_WIDTH)],
        out_shape=[jax.ShapeDtypeStruct((s, ATTN_WIDTH), BF16)] * 3
        + [jax.ShapeDtypeStruct((s, CONV_WIDTH), BF16)],
        scratch_shapes=[pltpu.VMEM((tm + SUBLANES, CONV_WIDTH), F32)],
        compiler_params=_params("arbitrary"),
        name="inproj",
    )(x, g, sc, sh, w_in, qg, kg, cos_t, sin_t, bd, conv_w)


def _attn_kernel(*refs, first, last):
    tq = BRANCH_KEYS
    if first:
        q_ref, kp_ref, kc_ref, vp_ref, vc_ref, acc_out, ml_out = refs
    elif last:
        q_ref, kp_ref, kc_ref, vp_ref, vc_ref, acc_in, ml_in, out_ref = refs
    else:
        q_ref, kp_ref, kc_ref, vp_ref, vc_ref, acc_in, ml_in, acc_out, ml_out = refs
    ib = pl.program_id(1)
    qi = lax.broadcasted_iota(jnp.int32, (tq, tq), 0)
    ki = lax.broadcasted_iota(jnp.int32, (tq, tq), 1)
    mask_c = ki <= qi
    mask_p = jnp.logical_and(ki >= qi, ib > 0)
    lane = ki
    ml_old = None if first else ml_in[...]
    heads = range(N_ATTN_HEADS)
    sls = [slice(HEAD_DIM * h, HEAD_DIM * (h + 1)) for h in heads]
    sp = [jnp.where(mask_p, _dot_nt(q_ref[:, sl], kp_ref[:, sl]), NEG_INF) for sl in sls]
    sc = [jnp.where(mask_c, _dot_nt(q_ref[:, sl], kc_ref[:, sl]), NEG_INF) for sl in sls]
    m_new = [jnp.max(jnp.maximum(a, b), axis=1, keepdims=True) for a, b in zip(sp, sc)]
    if not first:
        m_old = [jnp.max(jnp.where(lane == h, ml_old, NEG_INF), axis=1, keepdims=True) for h in heads]
        l_old = [jnp.sum(jnp.where(lane == 16 + h, ml_old, 0.0), axis=1, keepdims=True) for h in heads]
        m_new = [jnp.maximum(a, b) for a, b in zip(m_old, m_new)]
        alpha = [jnp.exp(a - b) for a, b in zip(m_old, m_new)]
    pp = [jnp.exp(s - m) for s, m in zip(sp, m_new)]
    pc = [jnp.exp(s - m) for s, m in zip(sc, m_new)]
    l_new = [jnp.sum(a + b, axis=1, keepdims=True) for a, b in zip(pp, pc)]
    acc = [_dot(a.astype(BF16), vp_ref[:, sl]) + _dot(b.astype(BF16), vc_ref[:, sl])
           for a, b, sl in zip(pp, pc, sls)]
    if not first:
        l_new = [a * lo + ln for a, lo, ln in zip(alpha, l_old, l_new)]
        acc = [a * acc_in[:, sl].astype(F32) + pv for a, sl, pv in zip(alpha, sls, acc)]
    if last:
        for sl, a, l in zip(sls, acc, l_new):
            out_ref[:, sl] = (a / l).astype(BF16)
    else:
        ml_new = jnp.zeros((tq, LANES), F32)
        for h, sl in zip(heads, sls):
            acc_out[:, sl] = acc[h].astype(BF16)
            ml_new = jnp.where(lane == h, m_new[h], ml_new)
            ml_new = jnp.where(lane == 16 + h, l_new[h], ml_new)
        ml_out[...] = ml_new


def _attn_pass(q, k, v, state, dil, first, last):
    s = q.shape[0]
    n = s // dil
    tq = BRANCH_KEYS
    view = lambda a: a.reshape(n, dil * a.shape[1])
    cur = lambda w: pl.BlockSpec((tq, w), lambda r, ib: (ib, r))
    prev = lambda w: pl.BlockSpec((tq, w), lambda r, ib: (jnp.maximum(ib - 1, 0), r))
    aw = ATTN_WIDTH
    in_specs = [cur(aw), prev(aw), cur(aw), prev(aw), cur(aw)]
    args = [view(q), view(k), view(k), view(v), view(v)]
    if not first:
        in_specs += [cur(aw), cur(LANES)]
        args += [view(state[0]), view(state[1])]
    if last:
        out_specs = [cur(aw)]
        out_shape = [jax.ShapeDtypeStruct((n, dil * aw), BF16)]
    else:
        out_specs = [cur(aw), cur(LANES)]
        out_shape = [jax.ShapeDtypeStruct((n, dil * aw), BF16), jax.ShapeDtypeStruct((n, dil * LANES), F32)]
    outs = pl.pallas_call(
        functools.partial(_attn_kernel, first=first, last=last),
        grid=(dil, n // tq),
        in_specs=in_specs, out_specs=out_specs, out_shape=out_shape,
        compiler_params=_params("parallel", "parallel"),
        name=f"attn_d{dil}",
    )(*args)
    return [o.reshape(s, -1) for o in outs]


def _attention(q, k, v):
    state = None
    for idx, dil in enumerate(DILATIONS):
        state = _attn_pass(q, k, v, state, dil, first=idx == 0, last=idx == len(DILATIONS) - 1)
    return state[0]


def _outproj_kernel(x_ref, attn_ref, conv_ref, woa_ref, woc_ref, gt_ref, g_ref, sc_ref, sh_ref, wq_ref,
                    keys_ref, x1_ref, h2_ref, st_ref):
    mix = _dot(attn_ref[...], woa_ref[...]) + _dot(conv_ref[...], woc_ref[...])
    x1 = x_ref[...] + gt_ref[...] * mix
    x1_ref[...] = x1
    h2 = _rms_mod(x1, g_ref[...], sc_ref[...], sh_ref[...]).astype(BF16)
    h2_ref[...] = h2
    qp = _dot(h2, wq_ref[...])
    for g in range(2 * PEER_HEADS):
        qg = qp[:, PEER_NKEYS * g:PEER_NKEYS * (g + 1)].astype(BF16)
        st_ref[PEER_NKEYS * g:PEER_NKEYS * (g + 1), :] = _dot_nt(keys_ref[g], qg)


def _outproj(x, attn, conv, wo_a, wo_c, gt, g, sc, sh, wq, keys, tm):
    s = x.shape[0]
    nq = wq.shape[1]
    row = lambda n: pl.BlockSpec((1, n), lambda i: (0, 0))
    tok = lambda n: pl.BlockSpec((tm, n), lambda i: (i, 0))
    full = lambda a: pl.BlockSpec(a.shape, lambda i: (0,) * a.ndim)
    return pl.pallas_call(
        _outproj_kernel,
        grid=(s // tm,),
        in_specs=[tok(D_MODEL), tok(ATTN_WIDTH), tok(CONV_WIDTH), full(wo_a), full(wo_c),
                  row(D_MODEL), row(D_MODEL), row(D_MODEL), row(D_MODEL), full(wq), full(keys)],
        out_specs=[tok(D_MODEL), tok(D_MODEL), pl.BlockSpec((nq, tm), lambda i: (0, i))],
        out_shape=[jax.ShapeDtypeStruct((s, D_MODEL), F32), jax.ShapeDtypeStruct((s, D_MODEL), BF16),
                   jax.ShapeDtypeStruct((nq, s), F32)],
        compiler_params=_params("parallel"),
        name="outproj",
    )(x, attn, conv, wo_a, wo_c, gt, g, sc, sh, wq, keys)


def _top16(ss, tb):
    n = ss[0].shape[0]
    row = lax.broadcasted_iota(jnp.int32, (n, tb), 0)
    row16 = lax.broadcasted_iota(jnp.int32, (PEER_TOPK, tb), 0)
    cur = list(ss)
    vals = [jnp.zeros((PEER_TOPK, tb), F32) for _ in ss]
    mark = lambda kk: -(F32_MAX - kk * F32_MAX_ULP)
    for kk in range(PEER_TOPK):
        m = [jnp.max(c, axis=0, keepdims=True) for c in cur]
        idx = [jnp.min(jnp.where(c == mm, row, n), axis=0, keepdims=True) for c, mm in zip(cur, m)]
        cur = [jnp.where(row == i, mark(kk), c) for i, c in zip(idx, cur)]
        vals = [jnp.where(row16 == kk, mm, v) for mm, v in zip(m, vals)]
    rank = [jnp.where(c <= mark(PEER_TOPK - 1), (c - mark(0)) * (1.0 / F32_MAX_ULP), float(PEER_TOPK))
            for c in cur]
    return vals, rank


def _topk_kernel(st_ref, rank2_ref, cnt_ref, e1_ref, e2_ref, *, tb):
    kk = PEER_TOPK
    s1 = st_ref[0:PEER_NKEYS, :]
    s2 = st_ref[PEER_NKEYS:2 * PEER_NKEYS, :]
    (v1, v2), (rank1, rank2) = _top16([s1, s2], tb)

    sub = lax.broadcasted_iota(jnp.int32, (SUBLANES, tb), 0)
    v2lo, v2hi = v2[0:SUBLANES, :], v2[SUBLANES:kk, :]
    groups = [(v1[0:1, :] + v2lo, sub, 0), (v1[0:1, :] + v2hi, sub + SUBLANES, 0),
              (v1[1:2, :] + v2lo, sub + kk, 1)]
    for a in range(2, SUBLANES):
        nb = kk // (a + 1)
        groups.append((jnp.where(sub < nb, v1[a:a + 1, :] + v2lo, NEG_INF), sub + kk * a, a))
    groups.append((v1[SUBLANES:kk, :] + v2[0:1, :], (sub + SUBLANES) * kk, None))
    cand = [g[0] for g in groups]
    flat = [g[1] for g in groups]
    cur = list(cand)
    big = kk * kk
    for _ in range(kk):
        m = functools.reduce(jnp.maximum, cur)
        m = jnp.max(m, axis=0, keepdims=True)
        hit = [jnp.where(c == m, f, big) for c, f in zip(cur, flat)]
        idx = jnp.min(functools.reduce(jnp.minimum, hit), axis=0, keepdims=True)
        cur = [jnp.where(f == idx, NEG_INF, c) for c, f in zip(cur, flat)]
    sel = [jnp.logical_and(c == NEG_INF, o != NEG_INF) for c, o in zip(cur, cand)]

    one = lambda mk: jnp.where(mk, 1.0, 0.0)
    counts = [jnp.sum(one(sel[0]) + one(sel[1]), axis=0, keepdims=True)]
    for gi in range(2, 2 + SUBLANES - 1):
        counts.append(jnp.sum(one(sel[gi]), axis=0, keepdims=True))
    tail = one(sel[-1])
    cnt = jnp.zeros((PEER_NKEYS, tb), F32)
    for a in range(kk):
        na = counts[a] if a < SUBLANES else tail[a - SUBLANES:a - SUBLANES + 1, :]
        cnt = jnp.where(rank1 == float(a), na, cnt)

    m1, m2 = v1[0:1, :], v2[0:1, :]
    mt = m1 + m2
    z = functools.reduce(lambda x, y: x + y,
                         [jnp.sum(jnp.where(sl, jnp.exp(o - mt), 0.0), axis=0, keepdims=True)
                          for sl, o in zip(sel, cand)])
    rank2_ref[...] = pltpu.bitcast(rank2.astype(BF16), jnp.uint32)
    cnt_ref[...] = cnt
    e1_ref[...] = jnp.exp(s1 - m1)
    e2_ref[...] = pltpu.bitcast((jnp.exp(s2 - m2) / z).astype(BF16), jnp.uint32)


def _topk(st, tb):
    s = st.shape[1]
    crow = PEER_NKEYS * SUBLANES // PACK
    plane = pl.BlockSpec((None, PEER_NKEYS, tb), lambda h, i: (h, 0, i))
    cplane = pl.BlockSpec((None, crow, tb), lambda h, i: (h, 0, i))
    shape = jax.ShapeDtypeStruct((PEER_HEADS, PEER_NKEYS, s), F32)
    cshape = jax.ShapeDtypeStruct((PEER_HEADS, crow, s), jnp.uint32)
    return pl.pallas_call(
        functools.partial(_topk_kernel, tb=tb),
        grid=(PEER_HEADS, s // tb),
        in_specs=[pl.BlockSpec((2 * PEER_NKEYS, tb), lambda h, i: (h, i))],
        out_specs=[cplane, plane, plane, cplane],
        out_shape=[cshape, shape, shape, cshape],
        compiler_params=_params("parallel", "parallel"),
        name="peer_topk",
    )(st)


PACK = 16
PEER_PIECES = 4


def _peer_kernel(h_ref, u_ref, vt_ref, rank2_ref, cnt_ref, e1_ref, e2_ref, x1_ref, gt_ref,
                 out_ref, acc_ref, p_ref, *, tm, ec):
    j = pl.program_id(1)
    rows = ec // PEER_NKEYS

    @pl.when(j == 0)
    def _():
        acc_ref[...] = jnp.zeros_like(acc_ref)

    crows = pl.ds(pl.multiple_of(j * rows, SUBLANES), rows)
    nt = tm // LANES
    tiles_per_piece = nt // PEER_PIECES
    er = ec // PEER_PIECES
    erw = er * SUBLANES // PACK
    zt = []
    for t in range(nt):
        if t % tiles_per_piece == 0:
            q = t // tiles_per_piece
            u_q = pltpu.bitcast(u_ref[erw * q:erw * (q + 1), :], BF16)
            zt.append(_dot_nt(u_q, h_ref[...]))
        tl = slice(LANES * t, LANES * (t + 1))
        cnt_rows = [cnt_ref[hh, crows, tl] for hh in range(PEER_HEADS)]
        e1_rows = [e1_ref[hh, crows, tl] for hh in range(PEER_HEADS)]
        for r in range(rows):
            w = [None] * (PEER_NKEYS // PACK)
            for hh in range(PEER_HEADS):
                cb = jnp.broadcast_to(cnt_rows[hh][r:r + 1, :], (PACK, LANES)).astype(BF16)
                eb = jnp.broadcast_to(e1_rows[hh][r:r + 1, :], (PACK, LANES)).astype(BF16)
                for g in range(PEER_NKEYS // PACK):
                    gs = slice(SUBLANES * g, SUBLANES * (g + 1))
                    rk = pltpu.bitcast(rank2_ref[hh, gs, tl], BF16)
                    e2 = pltpu.bitcast(e2_ref[hh, gs, tl], BF16)
                    term = jnp.where(rk < cb, e2, 0.0) * eb
                    w[g] = term if w[g] is None else w[g] + term
            p_ref[PEER_NKEYS * r:PEER_NKEYS * (r + 1), tl] = jnp.concatenate(w, axis=0)
    for q in range(PEER_PIECES):
        rs = slice(er * q, er * (q + 1))
        p_ref[rs, :] = p_ref[rs, :] * _gelu(zt[q].astype(BF16))
    acc_ref[...] += _dot(pltpu.bitcast(vt_ref[...], BF16), p_ref[...])

    @pl.when(j == pl.num_programs(1) - 1)
    def _():
        out_ref[...] = x1_ref[...] + gt_ref[...] * acc_ref[...].T


def _peer(h2, u, vt, planes, x1, gt, tm, ec):
    s = h2.shape[0]
    ne = vt.shape[1]
    assert ec == SUBLANES * PEER_NKEYS and (tm // LANES) % PEER_PIECES == 0, (ec, tm)
    plane = pl.BlockSpec((PEER_HEADS, PEER_NKEYS, tm), lambda i, j: (0, 0, i))
    cplane = pl.BlockSpec((PEER_HEADS, PEER_NKEYS * SUBLANES // PACK, tm), lambda i, j: (0, 0, i))
    tok = pl.BlockSpec((tm, D_MODEL), lambda i, j: (i, 0))
    return pl.pallas_call(
        functools.partial(_peer_kernel, tm=tm, ec=ec),
        grid=(s // tm, ne // ec),
        in_specs=[tok, pl.BlockSpec((ec * SUBLANES // PACK, D_MODEL), lambda i, j: (j, 0)),
                  pl.BlockSpec((D_MODEL * SUBLANES // PACK, ec), lambda i, j: (0, j)),
                  cplane, plane, plane, cplane, tok, pl.BlockSpec((1, D_MODEL), lambda i, j: (0, 0))],
        out_specs=tok,
        out_shape=jax.ShapeDtypeStruct((s, D_MODEL), F32),
        scratch_shapes=[pltpu.VMEM((D_MODEL, tm), F32), pltpu.VMEM((ec, tm), BF16)],
        compiler_params=_params("parallel", "arbitrary"),
        name="peer_dense",
    )(h2, u, vt, *planes, x1, gt)


def _words_kernel(x_ref, o_ref, *, transpose):
    x = x_ref[...]
    if transpose:
        x = x.T
    o_ref[...] = pltpu.bitcast(x.astype(BF16), jnp.uint32)


def _expert_table_words(tabs, layer, transpose):
    _, ne, d = tabs.shape
    blk = D_MODEL
    shrink = PACK // SUBLANES
    if transpose:
        out_shape, out_spec = (d // shrink, ne), pl.BlockSpec((d // shrink, blk), lambda i: (0, i))
    else:
        out_shape, out_spec = (ne // shrink, d), pl.BlockSpec((blk // shrink, d), lambda i: (i, 0))
    return pl.pallas_call(
        functools.partial(_words_kernel, transpose=transpose),
        grid=(ne // blk,),
        in_specs=[pl.BlockSpec((None, blk, d), lambda i: (layer, i, 0))],
        out_specs=out_spec,
        out_shape=jax.ShapeDtypeStruct(out_shape, jnp.uint32),
        compiler_params=_params("parallel"),
        name="expert_words_t" if transpose else "expert_words",
    )(tabs)


def _tile(s, want):
    t = min(want, s)
    assert s % t == 0, (s, t)
    return t


def _layer(x, mod, cos_t, sin_t, bd, norm_mix, norm_ffn, w_in, q_norm, k_norm, conv_w, w_out,
           peer_wq, peer_keys, peer_u_all, peer_v_all, layer):
    s = x.shape[0]
    sh1, sc1, gt1, sh2, sc2, gt2 = [m.reshape(1, D_MODEL) for m in jnp.split(mod, N_MOD)]
    tm = _tile(s, 512)
    qg = jnp.tile(q_norm, N_ATTN_HEADS).reshape(1, ATTN_WIDTH)
    kg = jnp.tile(k_norm, N_ATTN_HEADS).reshape(1, ATTN_WIDTH)
    q, k, v, conv = _inproj(x, norm_mix.reshape(1, D_MODEL), sc1, sh1, w_in.astype(BF16), qg, kg,
                            cos_t, sin_t, bd, conv_w, tm)
    attn = _attention(q, k, v)
    wo = w_out.astype(BF16)
    keys = peer_keys.reshape(2 * PEER_HEADS, PEER_NKEYS, -1).astype(BF16)
    x1, h2, st = _outproj(x, attn, conv, wo[:ATTN_WIDTH], wo[ATTN_WIDTH:], gt1,
                          norm_ffn.reshape(1, D_MODEL), sc2, sh2, peer_wq.astype(BF16), keys, tm)
    planes = _topk(st, _tile(s, 512))
    return _peer(h2, _expert_table_words(peer_u_all, layer, False), _expert_table_words(peer_v_all, layer, True),
                 planes, x1, gt2, _tile(s, 512), SUBLANES * PEER_NKEYS)


def kernel(x, c, positions, w_ada, b_ada, norm_mix, norm_ffn, w_in, q_norm, k_norm, conv_w, w_out,
           peer_wq, peer_keys, peer_u, peer_v):
    b, s, d = x.shape
    assert b == 1 and d == D_MODEL and s % (max(DILATIONS) * BRANCH_KEYS) == 0, x.shape
    depth = w_ada.shape[0]
    mod = _ada(c, w_ada, b_ada)
    cos_t, sin_t = _rope_tables(positions, _tile(s, 1024))
    blk = jnp.arange(256) // HEAD_DIM
    bd = (blk[:, None] == blk[None, :]).astype(BF16)
    y = x.reshape(s, d)
    for l in range(depth):
        y = _layer(y, mod[l], cos_t, sin_t, bd, norm_mix[l], norm_ffn[l], w_in[l], q_norm[l], k_norm[l],
                   conv_w[l], w_out[l], peer_wq[l], peer_keys[l], peer_u, peer_v, l)
    return y.reshape(b, s, d)
```

```python
import functools

import jax
import jax.numpy as jnp
from jax import lax
from jax.experimental import pallas as pl
from jax.experimental.pallas import tpu as pltpu

F32 = jnp.float32
BF16 = jnp.bfloat16

D_MODEL = 1024
HEAD_DIM = 64
N_ATTN_HEADS = 12
ATTN_WIDTH = N_ATTN_HEADS * HEAD_DIM
CONV_WIDTH = D_MODEL - ATTN_WIDTH
IN_PROJ_WIDTH = 3 * ATTN_WIDTH + 3 * CONV_WIDTH
BRANCH_KEYS = 128
DILATIONS = (16, 4, 1)
ROPE_THETA = 10000.0
EPS = 1e-6
PEER_HEADS = 8
PEER_NKEYS = 128
PEER_TOPK = 16
N_MOD = 6

LANES = 128
SUBLANES = 8
VMEM_LIMIT = 56 * 1024 * 1024

NEG_INF = float("-inf")
F32_MAX = float(jnp.finfo(jnp.float32).max)
F32_MAX_ULP = 2.0 ** 104


def _dot(a, b):
    return jnp.dot(a, b, preferred_element_type=F32)


def _dot_nt(a, b):
    return lax.dot_general(a, b, (((1,), (1,)), ((), ())), preferred_element_type=F32)


def _gelu(x):
    return 0.5 * x * (1.0 + lax.erf(x * (0.5 ** 0.5)))


def _params(*sem):
    return pltpu.CompilerParams(dimension_semantics=sem, vmem_limit_bytes=VMEM_LIMIT)


def _ada_kernel(c_ref, w_ref, b_ref, o_ref):
    o_ref[...] = jnp.dot(c_ref[...], w_ref[...], preferred_element_type=F32,
                         precision=lax.Precision.HIGHEST) + b_ref[...]


def _ada(c, w_ada, b_ada):
    depth = w_ada.shape[0]
    c8 = jnp.broadcast_to(c, (SUBLANES, D_MODEL))
    out = pl.pallas_call(
        _ada_kernel,
        grid=(depth, N_MOD),
        in_specs=[pl.BlockSpec((SUBLANES, D_MODEL), lambda l, j: (0, 0)),
                  pl.BlockSpec((None, D_MODEL, D_MODEL), lambda l, j: (l, 0, j)),
                  pl.BlockSpec((None, 1, D_MODEL), lambda l, j: (l, 0, j))],
        out_specs=pl.BlockSpec((None, SUBLANES, D_MODEL), lambda l, j: (l, 0, j)),
        out_shape=jax.ShapeDtypeStruct((depth, SUBLANES, N_MOD * D_MODEL), F32),
        compiler_params=_params("parallel", "parallel"),
        name="ada_mod",
    )(c8, w_ada, b_ada.reshape(depth, 1, N_MOD * D_MODEL))
    return out[:, 0, :]


def _rope_kernel(pos_ref, freq_ref, sign_ref, cos_ref, sin_ref):
    ang = pos_ref[...].astype(F32) * freq_ref[...]
    cos_ref[...] = jnp.cos(ang)
    sin_ref[...] = jnp.sin(ang) * sign_ref[...]


def _rope_tables(positions, tm):
    s = positions.shape[-1]
    half = HEAD_DIM // 2
    freq = ROPE_THETA ** (-jnp.arange(half, dtype=F32) / half)
    freq = jnp.tile(freq, LANES // half).reshape(1, LANES)
    sign = jnp.tile(jnp.concatenate([-jnp.ones((half,), F32), jnp.ones((half,), F32)]), LANES // HEAD_DIM)
    sign = sign.reshape(1, LANES)
    row = pl.BlockSpec((1, LANES), lambda i: (0, 0))
    return pl.pallas_call(
        _rope_kernel,
        grid=(s // tm,),
        in_specs=[pl.BlockSpec((tm, 1), lambda i: (i, 0)), row, row],
        out_specs=[pl.BlockSpec((tm, LANES), lambda i: (i, 0))] * 2,
        out_shape=[jax.ShapeDtypeStruct((s, LANES), F32)] * 2,
        compiler_params=_params("parallel"),
        name="rope_tables",
    )(positions.reshape(s, 1), freq, sign)


def _rms_mod(x, g, sc, sh):
    ms = jnp.mean(x * x, axis=-1, keepdims=True)
    y = x * lax.rsqrt(ms + EPS) * g
    return y * (1.0 + sc) + sh


def _inproj_kernel(x_ref, g_ref, sc_ref, sh_ref, w_ref, qg_ref, kg_ref, cos_ref, sin_ref, bd_ref, cw_ref,
                   q_ref, k_ref, v_ref, q16_ref, k16_ref, v16_ref, q4_ref, k4_ref, v4_ref, conv_ref,
                   ubuf_ref, stage_ref, *, tm):
    i = pl.program_id(0)
    stage_slot = [0]

    def emit(val, c0, nat_ref, strided_refs):
        nat_ref[:, c0:c0 + LANES] = val.astype(BF16)
        slot = stage_slot[0]
        stage_slot[0] += 1
        stage_ref[slot] = val
        for dil, ref in strided_refs:
            for r in range(dil):
                piece = stage_ref[slot, pl.ds(r, tm // dil, stride=dil), :]
                ref[:, r * ATTN_WIDTH + c0:r * ATTN_WIDTH + c0 + LANES] = piece.astype(BF16)
    h = _rms_mod(x_ref[...], g_ref[...], sc_ref[...], sh_ref[...])
    proj = _dot(h.astype(BF16), w_ref[...])
    cos = cos_ref[...]
    sin = sin_ref[...]
    lane = lax.broadcasted_iota(jnp.int32, (tm, LANES), 1)
    upper = (lane & (HEAD_DIM // 2)) != 0

    def head_norm_rope(z, gain_ref, out_refs, scale):
        for j in range(ATTN_WIDTH // 256):
            zj = z[:, 256 * j:256 * (j + 1)]
            ss = _dot((zj * zj).astype(BF16), bd_ref[...])
            zn = zj * lax.rsqrt(ss * (1.0 / HEAD_DIM) + EPS) * gain_ref[:, 256 * j:256 * (j + 1)]
            for t in range(2):
                zt = zn[:, LANES * t:LANES * (t + 1)]
                partner = jnp.where(upper, pltpu.roll(zt, HEAD_DIM // 2, 1),
                                    pltpu.roll(zt, LANES - HEAD_DIM // 2, 1))
                r = zt * cos + partner * sin
                c0 = 256 * j + LANES * t
                emit(r * scale, c0, out_refs[0], out_refs[1])

    a = ATTN_WIDTH
    d_hi, d_lo = DILATIONS[0], DILATIONS[1]
    head_norm_rope(proj[:, 0:a], qg_ref, (q_ref, ((d_hi, q16_ref), (d_lo, q4_ref))), HEAD_DIM ** -0.5)
    head_norm_rope(proj[:, a:2 * a], kg_ref, (k_ref, ((d_hi, k16_ref), (d_lo, k4_ref))), 1.0)
    for c0 in range(0, a, LANES):
        emit(proj[:, 2 * a + c0:2 * a + c0 + LANES], c0, v_ref, ((d_hi, v16_ref), (d_lo, v4_ref)))

    c = CONV_WIDTH
    bg = proj[:, 3 * a:3 * a + c]
    cg = proj[:, 3 * a + c:3 * a + 2 * c]
    xv = proj[:, 3 * a + 2 * c:3 * a + 3 * c]
    u = cg * xv

    @pl.when(i == 0)
    def _():
        ubuf_ref[0:SUBLANES, :] = jnp.zeros((SUBLANES, c), F32)

    ubuf_ref[SUBLANES:SUBLANES + tm, :] = u
    u1 = ubuf_ref[SUBLANES - 1:SUBLANES - 1 + tm, :]
    u2 = ubuf_ref[SUBLANES - 2:SUBLANES - 2 + tm, :]
    y = u2 * cw_ref[0:1, :] + u1 * cw_ref[1:2, :] + u * cw_ref[2:3, :]
    conv_ref[...] = (bg * y).astype(BF16)
    ubuf_ref[0:SUBLANES, :] = ubuf_ref[tm:tm + SUBLANES, :]


def _inproj(x, g, sc, sh, w_in, qg, kg, cos_t, sin_t, bd, conv_w, tm):
    s = x.shape[0]
    row = lambda n: pl.BlockSpec((1, n), lambda i: (0, 0))
    tok = lambda n: pl.BlockSpec((tm, n), lambda i: (i, 0))
    return pl.pallas_call(
        functools.partial(_inproj_kernel, tm=tm),
        grid=(s // tm,),
        in_specs=[tok(D_MODEL), row(D_MODEL), row(D_MODEL), row(D_MODEL),
                  pl.BlockSpec((D_MODEL, IN_PROJ_WIDTH), lambda i: (0, 0)),
                  row(ATTN_WIDTH), row(ATTN_WIDTH), tok(LANES), tok(LANES),
                  pl.BlockSpec((256, 256), lambda i: (0, 0)),
                  pl.BlockSpec((3, CONV_WIDTH), lambda i: (0, 0))],
        out_specs=[tok(ATTN_WIDTH)] * 3
        + [pl.BlockSpec((tm // d, d * ATTN_WIDTH), lambda i: (i, 0)) for d in DILATIONS[:2] for _ in range(3)]
        + [tok(CONV_WIDTH)],
        out_shape=[jax.ShapeDtypeStruct((s, ATTN_WIDTH), BF16)] * 3
        + [jax.ShapeDtypeStruct((s // d, d * ATTN_WIDTH), BF16) for d in DILATIONS[:2] for _ in range(3)]
        + [jax.ShapeDtypeStruct((s, CONV_WIDTH), BF16)],
        scratch_shapes=[pltpu.VMEM((tm + SUBLANES, CONV_WIDTH), F32),
                        pltpu.VMEM((3 * ATTN_WIDTH // LANES, tm, LANES), F32)],
        compiler_params=_params("arbitrary"),
        name="inproj",
    )(x, g, sc, sh, w_in, qg, kg, cos_t, sin_t, bd, conv_w)


def _attn_kernel(*refs, first, last):
    tq = BRANCH_KEYS
    if first:
        q_ref, kp_ref, kc_ref, vp_ref, vc_ref, acc_out, ml_out = refs
    elif last:
        q_ref, kp_ref, kc_ref, vp_ref, vc_ref, acc_in, ml_in, out_ref = refs
    else:
        q_ref, kp_ref, kc_ref, vp_ref, vc_ref, acc_in, ml_in, acc_out, ml_out = refs
    ib = pl.program_id(1)
    qi = lax.broadcasted_iota(jnp.int32, (tq, tq), 0)
    ki = lax.broadcasted_iota(jnp.int32, (tq, tq), 1)
    mask_c = ki <= qi
    mask_p = jnp.logical_and(ki >= qi, ib > 0)
    lane = ki
    ml_old = None if first else ml_in[...]
    heads = range(N_ATTN_HEADS)
    sls = [slice(HEAD_DIM * h, HEAD_DIM * (h + 1)) for h in heads]
    sp = [jnp.where(mask_p, _dot_nt(q_ref[:, sl], kp_ref[:, sl]), NEG_INF) for sl in sls]
    sc = [jnp.where(mask_c, _dot_nt(q_ref[:, sl], kc_ref[:, sl]), NEG_INF) for sl in sls]
    m_new = [jnp.max(jnp.maximum(a, b), axis=1, keepdims=True) for a, b in zip(sp, sc)]
    if not first:
        m_old = [jnp.max(jnp.where(lane == h, ml_old, NEG_INF), axis=1, keepdims=True) for h in heads]
        l_old = [jnp.sum(jnp.where(lane == 16 + h, ml_old, 0.0), axis=1, keepdims=True) for h in heads]
        m_new = [jnp.maximum(a, b) for a, b in zip(m_old, m_new)]
        alpha = [jnp.exp(a - b) for a, b in zip(m_old, m_new)]
    pp = [jnp.exp(s - m) for s, m in zip(sp, m_new)]
    pc = [jnp.exp(s - m) for s, m in zip(sc, m_new)]
    l_new = [jnp.sum(a + b, axis=1, keepdims=True) for a, b in zip(pp, pc)]
    acc = [_dot(a.astype(BF16), vp_ref[:, sl]) + _dot(b.astype(BF16), vc_ref[:, sl])
           for a, b, sl in zip(pp, pc, sls)]
    if not first:
        l_new = [a * lo + ln for a, lo, ln in zip(alpha, l_old, l_new)]
        acc = [a * acc_in[:, sl].astype(F32) + pv for a, sl, pv in zip(alpha, sls, acc)]
    if last:
        for sl, a, l in zip(sls, acc, l_new):
            out_ref[:, sl] = (a / l).astype(BF16)
    else:
        ml_new = jnp.zeros((tq, LANES), F32)
        for h, sl in zip(heads, sls):
            acc_out[:, sl] = acc[h].astype(BF16)
            ml_new = jnp.where(lane == h, m_new[h], ml_new)
            ml_new = jnp.where(lane == 16 + h, l_new[h], ml_new)
        ml_out[...] = ml_new


def _attn_pass(q, k, v, state, dil, first, last):
    n = q.shape[0]
    s = n * dil
    tq = BRANCH_KEYS
    view = lambda a: a.reshape(n, dil * a.shape[1])
    cur = lambda w: pl.BlockSpec((tq, w), lambda r, ib: (ib, r))
    prev = lambda w: pl.BlockSpec((tq, w), lambda r, ib: (jnp.maximum(ib - 1, 0), r))
    aw = ATTN_WIDTH
    in_specs = [cur(aw), prev(aw), cur(aw), prev(aw), cur(aw)]
    args = [q, k, k, v, v]
    if not first:
        in_specs += [cur(aw), cur(LANES)]
        args += [view(state[0]), view(state[1])]
    if last:
        out_specs = [cur(aw)]
        out_shape = [jax.ShapeDtypeStruct((n, dil * aw), BF16)]
    else:
        out_specs = [cur(aw), cur(LANES)]
        out_shape = [jax.ShapeDtypeStruct((n, dil * aw), BF16), jax.ShapeDtypeStruct((n, dil * LANES), F32)]
    outs = pl.pallas_call(
        functools.partial(_attn_kernel, first=first, last=last),
        grid=(dil, n // tq),
        in_specs=in_specs, out_specs=out_specs, out_shape=out_shape,
        compiler_params=_params("parallel", "parallel"),
        name=f"attn_d{dil}",
    )(*args)
    return [o.reshape(s, -1) for o in outs]


def _attention(qkv_views):
    state = None
    for idx, dil in enumerate(DILATIONS):
        state = _attn_pass(*qkv_views[dil], state, dil, first=idx == 0, last=idx == len(DILATIONS) - 1)
    return state[0]


def _outproj_kernel(x_ref, attn_ref, conv_ref, woa_ref, woc_ref, gt_ref, g_ref, sc_ref, sh_ref, wq_ref,
                    keys_ref, x1_ref, h2_ref, st_ref):
    mix = _dot(attn_ref[...], woa_ref[...]) + _dot(conv_ref[...], woc_ref[...])
    x1 = x_ref[...] + gt_ref[...] * mix
    x1_ref[...] = x1
    h2 = _rms_mod(x1, g_ref[...], sc_ref[...], sh_ref[...]).astype(BF16)
    h2_ref[...] = h2
    qp = _dot(h2, wq_ref[...])
    for g in range(2 * PEER_HEADS):
        qg = qp[:, PEER_NKEYS * g:PEER_NKEYS * (g + 1)].astype(BF16)
        st_ref[PEER_NKEYS * g:PEER_NKEYS * (g + 1), :] = _dot_nt(keys_ref[g], qg)


def _outproj(x, attn, conv, wo_a, wo_c, gt, g, sc, sh, wq, keys, tm):
    s = x.shape[0]
    nq = wq.shape[1]
    row = lambda n: pl.BlockSpec((1, n), lambda i: (0, 0))
    tok = lambda n: pl.BlockSpec((tm, n), lambda i: (i, 0))
    full = lambda a: pl.BlockSpec(a.shape, lambda i: (0,) * a.ndim)
    return pl.pallas_call(
        _outproj_kernel,
        grid=(s // tm,),
        in_specs=[tok(D_MODEL), tok(ATTN_WIDTH), tok(CONV_WIDTH), full(wo_a), full(wo_c),
                  row(D_MODEL), row(D_MODEL), row(D_MODEL), row(D_MODEL), full(wq), full(keys)],
        out_specs=[tok(D_MODEL), tok(D_MODEL), pl.BlockSpec((nq, tm), lambda i: (0, i))],
        out_shape=[jax.ShapeDtypeStruct((s, D_MODEL), F32), jax.ShapeDtypeStruct((s, D_MODEL), BF16),
                   jax.ShapeDtypeStruct((nq, s), F32)],
        compiler_params=_params("parallel"),
        name="outproj",
    )(x, attn, conv, wo_a, wo_c, gt, g, sc, sh, wq, keys)


def _top16(ss, tb):
    n = ss[0].shape[0]
    row = lax.broadcasted_iota(jnp.int32, (n, tb), 0)
    row16 = lax.broadcasted_iota(jnp.int32, (PEER_TOPK, tb), 0)
    cur = list(ss)
    vals = [jnp.zeros((PEER_TOPK, tb), F32) for _ in ss]
    mark = lambda kk: -(F32_MAX - kk * F32_MAX_ULP)
    for kk in range(PEER_TOPK):
        m = [jnp.max(c, axis=0, keepdims=True) for c in cur]
        idx = [jnp.min(jnp.where(c == mm, row, n), axis=0, keepdims=True) for c, mm in zip(cur, m)]
        cur = [jnp.where(row == i, mark(kk), c) for i, c in zip(idx, cur)]
        vals = [jnp.where(row16 == kk, mm, v) for mm, v in zip(m, vals)]
    rank = [jnp.where(c <= mark(PEER_TOPK - 1), (c - mark(0)) * (1.0 / F32_MAX_ULP), float(PEER_TOPK))
            for c in cur]
    return vals, rank


def _topk_kernel(st_ref, rank2_ref, cnt_ref, e1_ref, e2_ref, *, tb):
    kk = PEER_TOPK
    s1 = st_ref[0:PEER_NKEYS, :]
    s2 = st_ref[PEER_NKEYS:2 * PEER_NKEYS, :]
    (v1, v2), (rank1, rank2) = _top16([s1, s2], tb)

    sub = lax.broadcasted_iota(jnp.int32, (SUBLANES, tb), 0)
    v2lo, v2hi = v2[0:SUBLANES, :], v2[SUBLANES:kk, :]
    groups = [(v1[0:1, :] + v2lo, sub, 0), (v1[0:1, :] + v2hi, sub + SUBLANES, 0),
              (v1[1:2, :] + v2lo, sub + kk, 1)]
    for a in range(2, SUBLANES):
        nb = kk // (a + 1)
        groups.append((jnp.where(sub < nb, v1[a:a + 1, :] + v2lo, NEG_INF), sub + kk * a, a))
    groups.append((v1[SUBLANES:kk, :] + v2[0:1, :], (sub + SUBLANES) * kk, None))
    cand = [g[0] for g in groups]
    flat = [g[1] for g in groups]
    cur = list(cand)
    big = kk * kk
    for _ in range(kk):
        m = functools.reduce(jnp.maximum, cur)
        m = jnp.max(m, axis=0, keepdims=True)
        hit = [jnp.where(c == m, f, big) for c, f in zip(cur, flat)]
        idx = jnp.min(functools.reduce(jnp.minimum, hit), axis=0, keepdims=True)
        cur = [jnp.where(f == idx, NEG_INF, c) for c, f in zip(cur, flat)]
    sel = [jnp.logical_and(c == NEG_INF, o != NEG_INF) for c, o in zip(cur, cand)]

    one = lambda mk: jnp.where(mk, 1.0, 0.0)
    counts = [jnp.sum(one(sel[0]) + one(sel[1]), axis=0, keepdims=True)]
    for gi in range(2, 2 + SUBLANES - 1):
        counts.append(jnp.sum(one(sel[gi]), axis=0, keepdims=True))
    tail = one(sel[-1])
    cnt = jnp.zeros((PEER_NKEYS, tb), F32)
    for a in range(kk):
        na = counts[a] if a < SUBLANES else tail[a - SUBLANES:a - SUBLANES + 1, :]
        cnt = jnp.where(rank1 == float(a), na, cnt)

    m1, m2 = v1[0:1, :], v2[0:1, :]
    mt = m1 + m2
    z = functools.reduce(lambda x, y: x + y,
                         [jnp.sum(jnp.where(sl, jnp.exp(o - mt), 0.0), axis=0, keepdims=True)
                          for sl, o in zip(sel, cand)])
    rank2_ref[...] = pltpu.bitcast(rank2.astype(BF16), jnp.uint32)
    cnt_ref[...] = cnt
    e1_ref[...] = jnp.exp(s1 - m1)
    e2_ref[...] = pltpu.bitcast((jnp.exp(s2 - m2) / z).astype(BF16), jnp.uint32)


def _topk(st, tb):
    s = st.shape[1]
    crow = PEER_NKEYS * SUBLANES // PACK
    plane = pl.BlockSpec((None, PEER_NKEYS, tb), lambda h, i: (h, 0, i))
    cplane = pl.BlockSpec((None, crow, tb), lambda h, i: (h, 0, i))
    shape = jax.ShapeDtypeStruct((PEER_HEADS, PEER_NKEYS, s), F32)
    cshape = jax.ShapeDtypeStruct((PEER_HEADS, crow, s), jnp.uint32)
    return pl.pallas_call(
        functools.partial(_topk_kernel, tb=tb),
        grid=(PEER_HEADS, s // tb),
        in_specs=[pl.BlockSpec((2 * PEER_NKEYS, tb), lambda h, i: (h, i))],
        out_specs=[cplane, plane, plane, cplane],
        out_shape=[cshape, shape, shape, cshape],
        compiler_params=_params("parallel", "parallel"),
        name="peer_topk",
    )(st)


PACK = 16
PEER_PIECES = 4


def _peer_kernel(h_ref, u_ref, vt_ref, rank2_ref, cnt_ref, e1_ref, e2_ref, x1_ref, gt_ref,
                 out_ref, acc_ref, p_ref, *, tm, ec):
    j = pl.program_id(1)
    rows = ec // PEER_NKEYS

    @pl.when(j == 0)
    def _():
        acc_ref[...] = jnp.zeros_like(acc_ref)

    crows = pl.ds(pl.multiple_of(j * rows, SUBLANES), rows)
    nt = tm // LANES
    tiles_per_piece = nt // PEER_PIECES
    er = ec // PEER_PIECES
    erw = er * SUBLANES // PACK
    zt = []
    for t in range(nt):
        if t % tiles_per_piece == 0:
            q = t // tiles_per_piece
            u_q = pltpu.bitcast(u_ref[erw * q:erw * (q + 1), :], BF16)
            zt.append(_dot_nt(u_q, h_ref[...]))
        tl = slice(LANES * t, LANES * (t + 1))
        cnt_rows = [cnt_ref[hh, crows, tl] for hh in range(PEER_HEADS)]
        e1_rows = [e1_ref[hh, crows, tl] for hh in range(PEER_HEADS)]
        for r in range(rows):
            w = [None] * (PEER_NKEYS // PACK)
            for hh in range(PEER_HEADS):
                cb = jnp.broadcast_to(cnt_rows[hh][r:r + 1, :], (PACK, LANES)).astype(BF16)
                eb = jnp.broadcast_to(e1_rows[hh][r:r + 1, :], (PACK, LANES)).astype(BF16)
                for g in range(PEER_NKEYS // PACK):
                    gs = slice(SUBLANES * g, SUBLANES * (g + 1))
                    rk = pltpu.bitcast(rank2_ref[hh, gs, tl], BF16)
                    e2 = pltpu.bitcast(e2_ref[hh, gs, tl], BF16)
                    term = jnp.where(rk < cb, e2, 0.0) * eb
                    w[g] = term if w[g] is None else w[g] + term
            p_ref[PEER_NKEYS * r:PEER_NKEYS * (r + 1), tl] = jnp.concatenate(w, axis=0)
    for q in range(PEER_PIECES):
        rs = slice(er * q, er * (q + 1))
        p_ref[rs, :] = p_ref[rs, :] * _gelu(zt[q].astype(BF16))
    acc_ref[...] += _dot(pltpu.bitcast(vt_ref[...], BF16), p_ref[...])

    @pl.when(j == pl.num_programs(1) - 1)
    def _():
        out_ref[...] = x1_ref[...] + gt_ref[...] * acc_ref[...].T


def _peer(h2, u, vt, planes, x1, gt, tm, ec):
    s = h2.shape[0]
    ne = vt.shape[1]
    assert ec == SUBLANES * PEER_NKEYS and (tm // LANES) % PEER_PIECES == 0, (ec, tm)
    plane = pl.BlockSpec((PEER_HEADS, PEER_NKEYS, tm), lambda i, j: (0, 0, i))
    cplane = pl.BlockSpec((PEER_HEADS, PEER_NKEYS * SUBLANES // PACK, tm), lambda i, j: (0, 0, i))
    tok = pl.BlockSpec((tm, D_MODEL), lambda i, j: (i, 0))
    return pl.pallas_call(
        functools.partial(_peer_kernel, tm=tm, ec=ec),
        grid=(s // tm, ne // ec),
        in_specs=[tok, pl.BlockSpec((ec * SUBLANES // PACK, D_MODEL), lambda i, j: (j, 0)),
                  pl.BlockSpec((D_MODEL * SUBLANES // PACK, ec), lambda i, j: (0, j)),
                  cplane, plane, plane, cplane, tok, pl.BlockSpec((1, D_MODEL), lambda i, j: (0, 0))],
        out_specs=tok,
        out_shape=jax.ShapeDtypeStruct((s, D_MODEL), F32),
        scratch_shapes=[pltpu.VMEM((D_MODEL, tm), F32), pltpu.VMEM((ec, tm), BF16)],
        compiler_params=_params("parallel", "arbitrary"),
        name="peer_dense",
    )(h2, u, vt, *planes, x1, gt)


def _words_kernel(x_ref, o_ref, *, transpose):
    x = x_ref[...]
    if transpose:
        x = x.T
    o_ref[...] = pltpu.bitcast(x.astype(BF16), jnp.uint32)


def _expert_table_words(tabs, layer, transpose):
    _, ne, d = tabs.shape
    blk = D_MODEL
    shrink = PACK // SUBLANES
    if transpose:
        out_shape, out_spec = (d // shrink, ne), pl.BlockSpec((d // shrink, blk), lambda i: (0, i))
    else:
        out_shape, out_spec = (ne // shrink, d), pl.BlockSpec((blk // shrink, d), lambda i: (i, 0))
    return pl.pallas_call(
        functools.partial(_words_kernel, transpose=transpose),
        grid=(ne // blk,),
        in_specs=[pl.BlockSpec((None, blk, d), lambda i: (layer, i, 0))],
        out_specs=out_spec,
        out_shape=jax.ShapeDtypeStruct(out_shape, jnp.uint32),
        compiler_params=_params("parallel"),
        name="expert_words_t" if transpose else "expert_words",
    )(tabs)


def _tile(s, want):
    t = min(want, s)
    assert s % t == 0, (s, t)
    return t


def _layer(x, mod, cos_t, sin_t, bd, norm_mix, norm_ffn, w_in, q_norm, k_norm, conv_w, w_out,
           peer_wq, peer_keys, peer_u_all, peer_v_all, layer):
    s = x.shape[0]
    sh1, sc1, gt1, sh2, sc2, gt2 = [m.reshape(1, D_MODEL) for m in jnp.split(mod, N_MOD)]
    tm = _tile(s, 512)
    qg = jnp.tile(q_norm, N_ATTN_HEADS).reshape(1, ATTN_WIDTH)
    kg = jnp.tile(k_norm, N_ATTN_HEADS).reshape(1, ATTN_WIDTH)
    *qkv, conv = _inproj(x, norm_mix.reshape(1, D_MODEL), sc1, sh1, w_in.astype(BF16), qg, kg,
                         cos_t, sin_t, bd, conv_w, tm)
    views = {1: qkv[0:3], DILATIONS[0]: qkv[3:6], DILATIONS[1]: qkv[6:9]}
    attn = _attention(views)
    wo = w_out.astype(BF16)
    keys = peer_keys.reshape(2 * PEER_HEADS, PEER_NKEYS, -1).astype(BF16)
    x1, h2, st = _outproj(x, attn, conv, wo[:ATTN_WIDTH], wo[ATTN_WIDTH:], gt1,
                          norm_ffn.reshape(1, D_MODEL), sc2, sh2, peer_wq.astype(BF16), keys, tm)
    planes = _topk(st, _tile(s, 512))
    return _peer(h2, _expert_table_words(peer_u_all, layer, False), _expert_table_words(peer_v_all, layer, True),
                 planes, x1, gt2, _tile(s, 512), SUBLANES * PEER_NKEYS)


def kernel(x, c, positions, w_ada, b_ada, norm_mix, norm_ffn, w_in, q_norm, k_norm, conv_w, w_out,
           peer_wq, peer_keys, peer_u, peer_v):
    b, s, d = x.shape
    assert b == 1 and d == D_MODEL and s % (max(DILATIONS) * BRANCH_KEYS) == 0, x.shape
    depth = w_ada.shape[0]
    mod = _ada(c, w_ada, b_ada)
    cos_t, sin_t = _rope_tables(positions, _tile(s, 1024))
    blk = jnp.arange(256) // HEAD_DIM
    bd = (blk[:, None] == blk[None, :]).astype(BF16)
    y = x.reshape(s, d)
    for l in range(depth):
        y = _layer(y, mod[l], cos_t, sin_t, bd, norm_mix[l], norm_ffn[l], w_in[l], q_norm[l], k_norm[l],
                   conv_w[l], w_out[l], peer_wq[l], peer_keys[l], peer_u, peer_v, l)
    return y.reshape(b, s, d)
```

```python
import functools

import jax
import jax.numpy as jnp
from jax import lax
from jax.experimental import pallas as pl
from jax.experimental.pallas import tpu as pltpu

F32 = jnp.float32
BF16 = jnp.bfloat16

D_MODEL = 1024
HEAD_DIM = 64
N_ATTN_HEADS = 12
ATTN_WIDTH = N_ATTN_HEADS * HEAD_DIM
CONV_WIDTH = D_MODEL - ATTN_WIDTH
IN_PROJ_WIDTH = 3 * ATTN_WIDTH + 3 * CONV_WIDTH
BRANCH_KEYS = 128
DILATIONS = (16, 4, 1)
ROPE_THETA = 10000.0
EPS = 1e-6
PEER_HEADS = 8
PEER_NKEYS = 128
PEER_TOPK = 16
N_MOD = 6

LANES = 128
SUBLANES = 8
VMEM_LIMIT = 56 * 1024 * 1024

NEG_INF = float("-inf")
F32_MAX = float(jnp.finfo(jnp.float32).max)
F32_MAX_ULP = 2.0 ** 104


def _dot(a, b):
    return jnp.dot(a, b, preferred_element_type=F32)


def _dot_nt(a, b):
    return lax.dot_general(a, b, (((1,), (1,)), ((), ())), preferred_element_type=F32)


def _gelu(x):
    return 0.5 * x * (1.0 + lax.erf(x * (0.5 ** 0.5)))


def _params(*sem):
    return pltpu.CompilerParams(dimension_semantics=sem, vmem_limit_bytes=VMEM_LIMIT)


def _ada_kernel(c_ref, w_ref, b_ref, o_ref):
    o_ref[...] = jnp.dot(c_ref[...], w_ref[...], preferred_element_type=F32,
                         precision=lax.Precision.HIGHEST) + b_ref[...]


def _ada(c, w_ada, b_ada):
    depth = w_ada.shape[0]
    c8 = jnp.broadcast_to(c, (SUBLANES, D_MODEL))
    out = pl.pallas_call(
        _ada_kernel,
        grid=(depth, N_MOD),
        in_specs=[pl.BlockSpec((SUBLANES, D_MODEL), lambda l, j: (0, 0)),
                  pl.BlockSpec((None, D_MODEL, D_MODEL), lambda l, j: (l, 0, j)),
                  pl.BlockSpec((None, 1, D_MODEL), lambda l, j: (l, 0, j))],
        out_specs=pl.BlockSpec((None, SUBLANES, D_MODEL), lambda l, j: (l, 0, j)),
        out_shape=jax.ShapeDtypeStruct((depth, SUBLANES, N_MOD * D_MODEL), F32),
        compiler_params=_params("parallel", "parallel"),
        name="ada_mod",
    )(c8, w_ada, b_ada.reshape(depth, 1, N_MOD * D_MODEL))
    return out[:, 0, :]


def _rope_kernel(pos_ref, freq_ref, sign_ref, cos_ref, sin_ref):
    ang = pos_ref[...].astype(F32) * freq_ref[...]
    cos_ref[...] = jnp.cos(ang)
    sin_ref[...] = jnp.sin(ang) * sign_ref[...]


def _rope_tables(positions, tm):
    s = positions.shape[-1]
    half = HEAD_DIM // 2
    freq = ROPE_THETA ** (-jnp.arange(half, dtype=F32) / half)
    freq = jnp.tile(freq, LANES // half).reshape(1, LANES)
    sign = jnp.tile(jnp.concatenate([-jnp.ones((half,), F32), jnp.ones((half,), F32)]), LANES // HEAD_DIM)
    sign = sign.reshape(1, LANES)
    row = pl.BlockSpec((1, LANES), lambda i: (0, 0))
    return pl.pallas_call(
        _rope_kernel,
        grid=(s // tm,),
        in_specs=[pl.BlockSpec((tm, 1), lambda i: (i, 0)), row, row],
        out_specs=[pl.BlockSpec((tm, LANES), lambda i: (i, 0))] * 2,
        out_shape=[jax.ShapeDtypeStruct((s, LANES), F32)] * 2,
        compiler_params=_params("parallel"),
        name="rope_tables",
    )(positions.reshape(s, 1), freq, sign)


def _rms_mod(x, g, sc, sh):
    ms = jnp.mean(x * x, axis=-1, keepdims=True)
    y = x * lax.rsqrt(ms + EPS) * g
    return y * (1.0 + sc) + sh


def _inproj_kernel(x_ref, g_ref, sc_ref, sh_ref, w_ref, qg_ref, kg_ref, cos_ref, sin_ref, bd_ref, cw_ref,
                   q_ref, k_ref, v_ref, q16_ref, k16_ref, v16_ref, q4_ref, k4_ref, v4_ref, conv_ref,
                   ubuf_ref, stage_ref, *, tm):
    i = pl.program_id(0)
    stage_slot = [0]

    def emit(val, c0, nat_ref, strided_refs):
        nat_ref[:, c0:c0 + LANES] = val.astype(BF16)
        slot = stage_slot[0]
        stage_slot[0] += 1
        stage_ref[slot] = val
        for dil, ref in strided_refs:
            for r in range(dil):
                piece = stage_ref[slot, pl.ds(r, tm // dil, stride=dil), :]
                ref[:, r * ATTN_WIDTH + c0:r * ATTN_WIDTH + c0 + LANES] = piece.astype(BF16)
    h = _rms_mod(x_ref[...], g_ref[...], sc_ref[...], sh_ref[...])
    proj = _dot(h.astype(BF16), w_ref[...])
    cos = cos_ref[...]
    sin = sin_ref[...]
    lane = lax.broadcasted_iota(jnp.int32, (tm, LANES), 1)
    upper = (lane & (HEAD_DIM // 2)) != 0

    def head_norm_rope(z, gain_ref, out_refs, scale):
        for j in range(ATTN_WIDTH // 256):
            zj = z[:, 256 * j:256 * (j + 1)]
            ss = _dot((zj * zj).astype(BF16), bd_ref[...])
            zn = zj * lax.rsqrt(ss * (1.0 / HEAD_DIM) + EPS) * gain_ref[:, 256 * j:256 * (j + 1)]
            for t in range(2):
                zt = zn[:, LANES * t:LANES * (t + 1)]
                partner = jnp.where(upper, pltpu.roll(zt, HEAD_DIM // 2, 1),
                                    pltpu.roll(zt, LANES - HEAD_DIM // 2, 1))
                r = zt * cos + partner * sin
                c0 = 256 * j + LANES * t
                emit(r * scale, c0, out_refs[0], out_refs[1])

    a = ATTN_WIDTH
    d_hi, d_lo = DILATIONS[0], DILATIONS[1]
    head_norm_rope(proj[:, 0:a], qg_ref, (q_ref, ((d_hi, q16_ref), (d_lo, q4_ref))), HEAD_DIM ** -0.5)
    head_norm_rope(proj[:, a:2 * a], kg_ref, (k_ref, ((d_hi, k16_ref), (d_lo, k4_ref))), 1.0)
    for c0 in range(0, a, LANES):
        emit(proj[:, 2 * a + c0:2 * a + c0 + LANES], c0, v_ref, ((d_hi, v16_ref), (d_lo, v4_ref)))

    c = CONV_WIDTH
    bg = proj[:, 3 * a:3 * a + c]
    cg = proj[:, 3 * a + c:3 * a + 2 * c]
    xv = proj[:, 3 * a + 2 * c:3 * a + 3 * c]
    u = cg * xv

    @pl.when(i == 0)
    def _():
        ubuf_ref[0:SUBLANES, :] = jnp.zeros((SUBLANES, c), F32)

    ubuf_ref[SUBLANES:SUBLANES + tm, :] = u
    u1 = ubuf_ref[SUBLANES - 1:SUBLANES - 1 + tm, :]
    u2 = ubuf_ref[SUBLANES - 2:SUBLANES - 2 + tm, :]
    y = u2 * cw_ref[0:1, :] + u1 * cw_ref[1:2, :] + u * cw_ref[2:3, :]
    conv_ref[...] = (bg * y).astype(BF16)
    ubuf_ref[0:SUBLANES, :] = ubuf_ref[tm:tm + SUBLANES, :]


def _inproj(x, g, sc, sh, w_in, qg, kg, cos_t, sin_t, bd, conv_w, tm):
    s = x.shape[0]
    row = lambda n: pl.BlockSpec((1, n), lambda i: (0, 0))
    tok = lambda n: pl.BlockSpec((tm, n), lambda i: (i, 0))
    return pl.pallas_call(
        functools.partial(_inproj_kernel, tm=tm),
        grid=(s // tm,),
        in_specs=[tok(D_MODEL), row(D_MODEL), row(D_MODEL), row(D_MODEL),
                  pl.BlockSpec((D_MODEL, IN_PROJ_WIDTH), lambda i: (0, 0)),
                  row(ATTN_WIDTH), row(ATTN_WIDTH), tok(LANES), tok(LANES),
                  pl.BlockSpec((256, 256), lambda i: (0, 0)),
                  pl.BlockSpec((3, CONV_WIDTH), lambda i: (0, 0))],
        out_specs=[tok(ATTN_WIDTH)] * 3
        + [pl.BlockSpec((tm // d, d * ATTN_WIDTH), lambda i: (i, 0)) for d in DILATIONS[:2] for _ in range(3)]
        + [tok(CONV_WIDTH)],
        out_shape=[jax.ShapeDtypeStruct((s, ATTN_WIDTH), BF16)] * 3
        + [jax.ShapeDtypeStruct((s // d, d * ATTN_WIDTH), BF16) for d in DILATIONS[:2] for _ in range(3)]
        + [jax.ShapeDtypeStruct((s, CONV_WIDTH), BF16)],
        scratch_shapes=[pltpu.VMEM((tm + SUBLANES, CONV_WIDTH), F32),
                        pltpu.VMEM((3 * ATTN_WIDTH // LANES, tm, LANES), F32)],
        compiler_params=_params("arbitrary"),
        name="inproj",
    )(x, g, sc, sh, w_in, qg, kg, cos_t, sin_t, bd, conv_w)


def _attn_kernel(*refs, first, last):
    tq = BRANCH_KEYS
    if first:
        q_ref, kp_ref, kc_ref, vp_ref, vc_ref, acc_out, ml_out = refs
    elif last:
        q_ref, kp_ref, kc_ref, vp_ref, vc_ref, acc_in, ml_in, out_ref = refs
    else:
        q_ref, kp_ref, kc_ref, vp_ref, vc_ref, acc_in, ml_in, acc_out, ml_out = refs
    ib = pl.program_id(1)
    qi = lax.broadcasted_iota(jnp.int32, (tq, tq), 0)
    ki = lax.broadcasted_iota(jnp.int32, (tq, tq), 1)
    mask_c = ki <= qi
    mask_p = jnp.logical_and(ki >= qi, ib > 0)
    lane = ki
    ml_old = None if first else ml_in[...]
    heads = range(N_ATTN_HEADS)
    sls = [slice(HEAD_DIM * h, HEAD_DIM * (h + 1)) for h in heads]
    sp = [jnp.where(mask_p, _dot_nt(q_ref[:, sl], kp_ref[:, sl]), NEG_INF) for sl in sls]
    sc = [jnp.where(mask_c, _dot_nt(q_ref[:, sl], kc_ref[:, sl]), NEG_INF) for sl in sls]
    m_new = [jnp.max(jnp.maximum(a, b), axis=1, keepdims=True) for a, b in zip(sp, sc)]
    if not first:
        m_old = [jnp.max(jnp.where(lane == h, ml_old, NEG_INF), axis=1, keepdims=True) for h in heads]
        l_old = [jnp.sum(jnp.where(lane == 16 + h, ml_old, 0.0), axis=1, keepdims=True) for h in heads]
        m_new = [jnp.maximum(a, b) for a, b in zip(m_old, m_new)]
        alpha = [jnp.exp(a - b) for a, b in zip(m_old, m_new)]
    pp = [jnp.exp(s - m) for s, m in zip(sp, m_new)]
    pc = [jnp.exp(s - m) for s, m in zip(sc, m_new)]
    l_new = [jnp.sum(a + b, axis=1, keepdims=True) for a, b in zip(pp, pc)]
    acc = [_dot(a.astype(BF16), vp_ref[:, sl]) + _dot(b.astype(BF16), vc_ref[:, sl])
           for a, b, sl in zip(pp, pc, sls)]
    if not first:
        l_new = [a * lo + ln for a, lo, ln in zip(alpha, l_old, l_new)]
        acc = [a * acc_in[:, sl].astype(F32) + pv for a, sl, pv in zip(alpha, sls, acc)]
    if last:
        for sl, a, l in zip(sls, acc, l_new):
            out_ref[:, sl] = (a / l).astype(BF16)
    else:
        ml_new = jnp.zeros((tq, LANES), F32)
        for h, sl in zip(heads, sls):
            acc_out[:, sl] = acc[h].astype(BF16)
            ml_new = jnp.where(lane == h, m_new[h], ml_new)
            ml_new = jnp.where(lane == 16 + h, l_new[h], ml_new)
        ml_out[...] = ml_new


def _attn_pass(q, k, v, state, dil, first, last):
    n = q.shape[0]
    s = n * dil
    tq = BRANCH_KEYS
    view = lambda a: a.reshape(n, dil * a.shape[1])
    cur = lambda w: pl.BlockSpec((tq, w), lambda r, ib: (ib, r))
    prev = lambda w: pl.BlockSpec((tq, w), lambda r, ib: (jnp.maximum(ib - 1, 0), r))
    aw = ATTN_WIDTH
    in_specs = [cur(aw), prev(aw), cur(aw), prev(aw), cur(aw)]
    args = [q, k, k, v, v]
    if not first:
        in_specs += [cur(aw), cur(LANES)]
        args += [view(state[0]), view(state[1])]
    if last:
        out_specs = [cur(aw)]
        out_shape = [jax.ShapeDtypeStruct((n, dil * aw), BF16)]
    else:
        out_specs = [cur(aw), cur(LANES)]
        out_shape = [jax.ShapeDtypeStruct((n, dil * aw), BF16), jax.ShapeDtypeStruct((n, dil * LANES), F32)]
    outs = pl.pallas_call(
        functools.partial(_attn_kernel, first=first, last=last),
        grid=(dil, n // tq),
        in_specs=in_specs, out_specs=out_specs, out_shape=out_shape,
        compiler_params=_params("parallel", "parallel"),
        name=f"attn_d{dil}",
    )(*args)
    return [o.reshape(s, -1) for o in outs]


def _attention(qkv_views):
    state = None
    for idx, dil in enumerate(DILATIONS):
        state = _attn_pass(*qkv_views[dil], state, dil, first=idx == 0, last=idx == len(DILATIONS) - 1)
    return state[0]


def _outproj_kernel(x_ref, attn_ref, conv_ref, woa_ref, woc_ref, gt_ref, g_ref, sc_ref, sh_ref, wq_ref,
                    keys_ref, x1_ref, h2_ref, st_ref):
    mix = _dot(attn_ref[...], woa_ref[...]) + _dot(conv_ref[...], woc_ref[...])
    x1 = x_ref[...] + gt_ref[...] * mix
    x1_ref[...] = x1
    h2 = _rms_mod(x1, g_ref[...], sc_ref[...], sh_ref[...]).astype(BF16)
    h2_ref[...] = h2
    qp = _dot(h2, wq_ref[...])
    for g in range(2 * PEER_HEADS):
        qg = qp[:, PEER_NKEYS * g:PEER_NKEYS * (g + 1)].astype(BF16)
        st_ref[PEER_NKEYS * g:PEER_NKEYS * (g + 1), :] = _dot_nt(keys_ref[g], qg)


def _outproj(x, attn, conv, wo_a, wo_c, gt, g, sc, sh, wq, keys, tm):
    s = x.shape[0]
    nq = wq.shape[1]
    row = lambda n: pl.BlockSpec((1, n), lambda i: (0, 0))
    tok = lambda n: pl.BlockSpec((tm, n), lambda i: (i, 0))
    full = lambda a: pl.BlockSpec(a.shape, lambda i: (0,) * a.ndim)
    return pl.pallas_call(
        _outproj_kernel,
        grid=(s // tm,),
        in_specs=[tok(D_MODEL), tok(ATTN_WIDTH), tok(CONV_WIDTH), full(wo_a), full(wo_c),
                  row(D_MODEL), row(D_MODEL), row(D_MODEL), row(D_MODEL), full(wq), full(keys)],
        out_specs=[tok(D_MODEL), tok(D_MODEL), pl.BlockSpec((nq, tm), lambda i: (0, i))],
        out_shape=[jax.ShapeDtypeStruct((s, D_MODEL), F32), jax.ShapeDtypeStruct((s, D_MODEL), BF16),
                   jax.ShapeDtypeStruct((nq, s), F32)],
        compiler_params=_params("parallel"),
        name="outproj",
    )(x, attn, conv, wo_a, wo_c, gt, g, sc, sh, wq, keys)


def _top16(ss, tb):
    n = ss[0].shape[0]
    row = lax.broadcasted_iota(jnp.int32, (n, tb), 0)
    row16 = lax.broadcasted_iota(jnp.int32, (PEER_TOPK, tb), 0)
    cur = list(ss)
    vals = [jnp.zeros((PEER_TOPK, tb), F32) for _ in ss]
    mark = lambda kk: -(F32_MAX - kk * F32_MAX_ULP)
    for kk in range(PEER_TOPK):
        m = [jnp.max(c, axis=0, keepdims=True) for c in cur]
        idx = [jnp.min(jnp.where(c == mm, row, n), axis=0, keepdims=True) for c, mm in zip(cur, m)]
        cur = [jnp.where(row == i, mark(kk), c) for i, c in zip(idx, cur)]
        vals = [jnp.where(row16 == kk, mm, v) for mm, v in zip(m, vals)]
    rank = [jnp.where(c <= mark(PEER_TOPK - 1), (c - mark(0)) * (1.0 / F32_MAX_ULP), float(PEER_TOPK))
            for c in cur]
    return vals, rank


def _topk_kernel(st_ref, rank2_ref, cnt_ref, e1_ref, e2_ref, *, tb):
    kk = PEER_TOPK
    s1 = st_ref[0:PEER_NKEYS, :]
    s2 = st_ref[PEER_NKEYS:2 * PEER_NKEYS, :]
    (v1, v2), (rank1, rank2) = _top16([s1, s2], tb)

    sub = lax.broadcasted_iota(jnp.int32, (SUBLANES, tb), 0)
    v2lo, v2hi = v2[0:SUBLANES, :], v2[SUBLANES:kk, :]
    groups = [(v1[0:1, :] + v2lo, sub, 0), (v1[0:1, :] + v2hi, sub + SUBLANES, 0),
              (v1[1:2, :] + v2lo, sub + kk, 1)]
    for a in range(2, SUBLANES):
        nb = kk // (a + 1)
        groups.append((jnp.where(sub < nb, v1[a:a + 1, :] + v2lo, NEG_INF), sub + kk * a, a))
    groups.append((v1[SUBLANES:kk, :] + v2[0:1, :], (sub + SUBLANES) * kk, None))
    cand = [g[0] for g in groups]
    flat = [g[1] for g in groups]
    cur = list(cand)
    big = kk * kk
    for _ in range(kk):
        m = functools.reduce(jnp.maximum, cur)
        m = jnp.max(m, axis=0, keepdims=True)
        hit = [jnp.where(c == m, f, big) for c, f in zip(cur, flat)]
        idx = jnp.min(functools.reduce(jnp.minimum, hit), axis=0, keepdims=True)
        cur = [jnp.where(f == idx, NEG_INF, c) for c, f in zip(cur, flat)]
    sel = [jnp.logical_and(c == NEG_INF, o != NEG_INF) for c, o in zip(cur, cand)]

    one = lambda mk: jnp.where(mk, 1.0, 0.0)
    counts = [jnp.sum(one(sel[0]) + one(sel[1]), axis=0, keepdims=True)]
    for gi in range(2, 2 + SUBLANES - 1):
        counts.append(jnp.sum(one(sel[gi]), axis=0, keepdims=True))
    tail = one(sel[-1])
    cnt = jnp.zeros((PEER_NKEYS, tb), F32)
    for a in range(kk):
        na = counts[a] if a < SUBLANES else tail[a - SUBLANES:a - SUBLANES + 1, :]
        cnt = jnp.where(rank1 == float(a), na, cnt)

    m1, m2 = v1[0:1, :], v2[0:1, :]
    mt = m1 + m2
    z = functools.reduce(lambda x, y: x + y,
                         [jnp.sum(jnp.where(sl, jnp.exp(o - mt), 0.0), axis=0, keepdims=True)
                          for sl, o in zip(sel, cand)])
    rank2_ref[...] = pltpu.bitcast(rank2.astype(BF16), jnp.uint32)
    cnt_ref[...] = cnt
    e1_ref[...] = jnp.exp(s1 - m1)
    e2_ref[...] = pltpu.bitcast((jnp.exp(s2 - m2) / z).astype(BF16), jnp.uint32)


def _topk(st, tb):
    s = st.shape[1]
    crow = PEER_NKEYS * SUBLANES // PACK
    plane = pl.BlockSpec((None, PEER_NKEYS, tb), lambda h, i: (h, 0, i))
    cplane = pl.BlockSpec((None, crow, tb), lambda h, i: (h, 0, i))
    shape = jax.ShapeDtypeStruct((PEER_HEADS, PEER_NKEYS, s), F32)
    cshape = jax.ShapeDtypeStruct((PEER_HEADS, crow, s), jnp.uint32)
    return pl.pallas_call(
        functools.partial(_topk_kernel, tb=tb),
        grid=(PEER_HEADS, s // tb),
        in_specs=[pl.BlockSpec((2 * PEER_NKEYS, tb), lambda h, i: (h, i))],
        out_specs=[cplane, plane, plane, cplane],
        out_shape=[cshape, shape, shape, cshape],
        compiler_params=_params("parallel", "parallel"),
        name="peer_topk",
    )(st)


PACK = 16
PEER_PIECES = 4


def _peer_kernel(h_ref, u_ref, vt_ref, rank2_ref, cnt_ref, e1_ref, e2_ref, x1_ref, gt_ref,
                 out_ref, acc_ref, p_ref, *, tm, ec):
    j = pl.program_id(1)
    rows = ec // PEER_NKEYS

    @pl.when(j == 0)
    def _():
        acc_ref[...] = jnp.zeros_like(acc_ref)

    crows = pl.ds(pl.multiple_of(j * rows, SUBLANES), rows)
    nt = tm // LANES
    tiles_per_piece = nt // PEER_PIECES
    er = ec // PEER_PIECES
    erw = er * SUBLANES // PACK
    zt = []
    for t in range(nt):
        if t % tiles_per_piece == 0:
            q = t // tiles_per_piece
            u_q = pltpu.bitcast(u_ref[erw * q:erw * (q + 1), :], BF16)
            zt.append(_dot_nt(u_q, h_ref[...]))
        tl = slice(LANES * t, LANES * (t + 1))
        cnt_rows = [cnt_ref[hh, crows, tl] for hh in range(PEER_HEADS)]
        e1_rows = [e1_ref[hh, crows, tl] for hh in range(PEER_HEADS)]
        for r in range(rows):
            w = [None] * (PEER_NKEYS // PACK)
            for hh in range(PEER_HEADS):
                cb = jnp.broadcast_to(cnt_rows[hh][r:r + 1, :], (PACK, LANES)).astype(BF16)
                eb = jnp.broadcast_to(e1_rows[hh][r:r + 1, :], (PACK, LANES)).astype(BF16)
                for g in range(PEER_NKEYS // PACK):
                    gs = slice(SUBLANES * g, SUBLANES * (g + 1))
                    rk = pltpu.bitcast(rank2_ref[hh, gs, tl], BF16)
                    e2 = pltpu.bitcast(e2_ref[hh, gs, tl], BF16)
                    term = jnp.where(rk < cb, e2, 0.0) * eb
                    w[g] = term if w[g] is None else w[g] + term
            p_ref[PEER_NKEYS * r:PEER_NKEYS * (r + 1), tl] = jnp.concatenate(w, axis=0)
    for q in range(PEER_PIECES):
        rs = slice(er * q, er * (q + 1))
        p_ref[rs, :] = p_ref[rs, :] * _gelu(zt[q].astype(BF16))
    acc_ref[...] += _dot(pltpu.bitcast(vt_ref[...], BF16), p_ref[...])

    @pl.when(j == pl.num_programs(1) - 1)
    def _():
        out_ref[...] = x1_ref[...] + gt_ref[...] * acc_ref[...].T


def _peer(h2, u, vt, planes, x1, gt, tm, ec):
    s = h2.shape[0]
    ne = vt.shape[1]
    assert ec % (SUBLANES * PEER_NKEYS) == 0 and (tm // LANES) % PEER_PIECES == 0, (ec, tm)
    plane = pl.BlockSpec((PEER_HEADS, PEER_NKEYS, tm), lambda i, j: (0, 0, i))
    cplane = pl.BlockSpec((PEER_HEADS, PEER_NKEYS * SUBLANES // PACK, tm), lambda i, j: (0, 0, i))
    tok = pl.BlockSpec((tm, D_MODEL), lambda i, j: (i, 0))
    return pl.pallas_call(
        functools.partial(_peer_kernel, tm=tm, ec=ec),
        grid=(s // tm, ne // ec),
        in_specs=[tok, pl.BlockSpec((ec * SUBLANES // PACK, D_MODEL), lambda i, j: (j, 0)),
                  pl.BlockSpec((D_MODEL * SUBLANES // PACK, ec), lambda i, j: (0, j)),
                  cplane, plane, plane, cplane, tok, pl.BlockSpec((1, D_MODEL), lambda i, j: (0, 0))],
        out_specs=tok,
        out_shape=jax.ShapeDtypeStruct((s, D_MODEL), F32),
        scratch_shapes=[pltpu.VMEM((D_MODEL, tm), F32), pltpu.VMEM((ec, tm), BF16)],
        compiler_params=_params("parallel", "arbitrary"),
        name="peer_dense",
    )(h2, u, vt, *planes, x1, gt)


def _words_kernel(x_ref, o_ref, *, transpose):
    x = x_ref[...]
    if transpose:
        x = x.T
    o_ref[...] = pltpu.bitcast(x.astype(BF16), jnp.uint32)


def _expert_table_words(tabs, layer, transpose):
    _, ne, d = tabs.shape
    blk = D_MODEL
    shrink = PACK // SUBLANES
    if transpose:
        out_shape, out_spec = (d // shrink, ne), pl.BlockSpec((d // shrink, blk), lambda i: (0, i))
    else:
        out_shape, out_spec = (ne // shrink, d), pl.BlockSpec((blk // shrink, d), lambda i: (i, 0))
    return pl.pallas_call(
        functools.partial(_words_kernel, transpose=transpose),
        grid=(ne // blk,),
        in_specs=[pl.BlockSpec((None, blk, d), lambda i: (layer, i, 0))],
        out_specs=out_spec,
        out_shape=jax.ShapeDtypeStruct(out_shape, jnp.uint32),
        compiler_params=_params("parallel"),
        name="expert_words_t" if transpose else "expert_words",
    )(tabs)


def _tile(s, want):
    t = min(want, s)
    assert s % t == 0, (s, t)
    return t


def _layer(x, mod, cos_t, sin_t, bd, norm_mix, norm_ffn, w_in, q_norm, k_norm, conv_w, w_out,
           peer_wq, peer_keys, peer_u_all, peer_v_all, layer):
    s = x.shape[0]
    sh1, sc1, gt1, sh2, sc2, gt2 = [m.reshape(1, D_MODEL) for m in jnp.split(mod, N_MOD)]
    tm = _tile(s, 512)
    qg = jnp.tile(q_norm, N_ATTN_HEADS).reshape(1, ATTN_WIDTH)
    kg = jnp.tile(k_norm, N_ATTN_HEADS).reshape(1, ATTN_WIDTH)
    *qkv, conv = _inproj(x, norm_mix.reshape(1, D_MODEL), sc1, sh1, w_in.astype(BF16), qg, kg,
                         cos_t, sin_t, bd, conv_w, tm)
    views = {1: qkv[0:3], DILATIONS[0]: qkv[3:6], DILATIONS[1]: qkv[6:9]}
    attn = _attention(views)
    wo = w_out.astype(BF16)
    keys = peer_keys.reshape(2 * PEER_HEADS, PEER_NKEYS, -1).astype(BF16)
    x1, h2, st = _outproj(x, attn, conv, wo[:ATTN_WIDTH], wo[ATTN_WIDTH:], gt1,
                          norm_ffn.reshape(1, D_MODEL), sc2, sh2, peer_wq.astype(BF16), keys, tm)
    planes = _topk(st, _tile(s, 512))
    return _peer(h2, _expert_table_words(peer_u_all, layer, False), _expert_table_words(peer_v_all, layer, True),
                 planes, x1, gt2, _tile(s, 512), 2 * SUBLANES * PEER_NKEYS)


def kernel(x, c, positions, w_ada, b_ada, norm_mix, norm_ffn, w_in, q_norm, k_norm, conv_w, w_out,
           peer_wq, peer_keys, peer_u, peer_v):
    b, s, d = x.shape
    assert b == 1 and d == D_MODEL and s % (max(DILATIONS) * BRANCH_KEYS) == 0, x.shape
    depth = w_ada.shape[0]
    mod = _ada(c, w_ada, b_ada)
    cos_t, sin_t = _rope_tables(positions, _tile(s, 1024))
    blk = jnp.arange(256) // HEAD_DIM
    bd = (blk[:, None] == blk[None, :]).astype(BF16)
    y = x.reshape(s, d)
    for l in range(depth):
        y = _layer(y, mod[l], cos_t, sin_t, bd, norm_mix[l], norm_ffn[l], w_in[l], q_norm[l], k_norm[l],
                   conv_w[l], w_out[l], peer_wq[l], peer_keys[l], peer_u, peer_v, l)
    return y.reshape(b, s, d)
```

```python
import functools

import jax
import jax.numpy as jnp
from jax import lax
from jax.experimental import pallas as pl
from jax.experimental.pallas import tpu as pltpu

F32 = jnp.float32
BF16 = jnp.bfloat16

D_MODEL = 1024
HEAD_DIM = 64
N_ATTN_HEADS = 12
ATTN_WIDTH = N_ATTN_HEADS * HEAD_DIM
CONV_WIDTH = D_MODEL - ATTN_WIDTH
IN_PROJ_WIDTH = 3 * ATTN_WIDTH + 3 * CONV_WIDTH
BRANCH_KEYS = 128
DILATIONS = (16, 4, 1)
ROPE_THETA = 10000.0
EPS = 1e-6
PEER_HEADS = 8
PEER_NKEYS = 128
PEER_TOPK = 16
N_MOD = 6

LANES = 128
SUBLANES = 8
VMEM_LIMIT = 56 * 1024 * 1024

NEG_INF = float("-inf")
F32_MAX = float(jnp.finfo(jnp.float32).max)
F32_MAX_ULP = 2.0 ** 104


def _dot(a, b):
    return jnp.dot(a, b, preferred_element_type=F32)


def _dot_nt(a, b):
    return lax.dot_general(a, b, (((1,), (1,)), ((), ())), preferred_element_type=F32)


def _gelu(x):
    return 0.5 * x * (1.0 + lax.erf(x * (0.5 ** 0.5)))


def _params(*sem):
    return pltpu.CompilerParams(dimension_semantics=sem, vmem_limit_bytes=VMEM_LIMIT)


def _ada_kernel(c_ref, w_ref, b_ref, o_ref):
    o_ref[...] = jnp.dot(c_ref[...], w_ref[...], preferred_element_type=F32,
                         precision=lax.Precision.HIGHEST) + b_ref[...]


def _ada(c, w_ada, b_ada):
    depth = w_ada.shape[0]
    c8 = jnp.broadcast_to(c, (SUBLANES, D_MODEL))
    out = pl.pallas_call(
        _ada_kernel,
        grid=(depth, N_MOD),
        in_specs=[pl.BlockSpec((SUBLANES, D_MODEL), lambda l, j: (0, 0)),
                  pl.BlockSpec((None, D_MODEL, D_MODEL), lambda l, j: (l, 0, j)),
                  pl.BlockSpec((None, 1, D_MODEL), lambda l, j: (l, 0, j))],
        out_specs=pl.BlockSpec((None, SUBLANES, D_MODEL), lambda l, j: (l, 0, j)),
        out_shape=jax.ShapeDtypeStruct((depth, SUBLANES, N_MOD * D_MODEL), F32),
        compiler_params=_params("parallel", "parallel"),
        name="ada_mod",
    )(c8, w_ada, b_ada.reshape(depth, 1, N_MOD * D_MODEL))
    return out[:, 0, :]


def _rope_kernel(pos_ref, freq_ref, sign_ref, cos_ref, sin_ref):
    ang = pos_ref[...].astype(F32) * freq_ref[...]
    cos_ref[...] = jnp.cos(ang)
    sin_ref[...] = jnp.sin(ang) * sign_ref[...]


def _rope_tables(positions, tm):
    s = positions.shape[-1]
    half = HEAD_DIM // 2
    freq = ROPE_THETA ** (-jnp.arange(half, dtype=F32) / half)
    freq = jnp.tile(freq, LANES // half).reshape(1, LANES)
    sign = jnp.tile(jnp.concatenate([-jnp.ones((half,), F32), jnp.ones((half,), F32)]), LANES // HEAD_DIM)
    sign = sign.reshape(1, LANES)
    row = pl.BlockSpec((1, LANES), lambda i: (0, 0))
    return pl.pallas_call(
        _rope_kernel,
        grid=(s // tm,),
        in_specs=[pl.BlockSpec((tm, 1), lambda i: (i, 0)), row, row],
        out_specs=[pl.BlockSpec((tm, LANES), lambda i: (i, 0))] * 2,
        out_shape=[jax.ShapeDtypeStruct((s, LANES), F32)] * 2,
        compiler_params=_params("parallel"),
        name="rope_tables",
    )(positions.reshape(s, 1), freq, sign)


def _rms_mod(x, g, sc, sh):
    ms = jnp.mean(x * x, axis=-1, keepdims=True)
    y = x * lax.rsqrt(ms + EPS) * g
    return y * (1.0 + sc) + sh


def _inproj_kernel(x_ref, g_ref, sc_ref, sh_ref, w_ref, qg_ref, kg_ref, cos_ref, sin_ref, bd_ref, cw_ref,
                   q_ref, k_ref, v_ref, q16_ref, k16_ref, v16_ref, q4_ref, k4_ref, v4_ref, conv_ref,
                   ubuf_ref, stage_ref, *, tm):
    i = pl.program_id(0)
    stage_slot = [0]

    def emit(val, c0, nat_ref, strided_refs):
        nat_ref[:, c0:c0 + LANES] = val.astype(BF16)
        slot = stage_slot[0]
        stage_slot[0] += 1
        stage_ref[slot] = val
        for dil, ref in strided_refs:
            for r in range(dil):
                piece = stage_ref[slot, pl.ds(r, tm // dil, stride=dil), :]
                ref[:, r * ATTN_WIDTH + c0:r * ATTN_WIDTH + c0 + LANES] = piece.astype(BF16)
    h = _rms_mod(x_ref[...], g_ref[...], sc_ref[...], sh_ref[...])
    proj = _dot(h.astype(BF16), w_ref[...])
    cos = cos_ref[...]
    sin = sin_ref[...]
    lane = lax.broadcasted_iota(jnp.int32, (tm, LANES), 1)
    upper = (lane & (HEAD_DIM // 2)) != 0

    def head_norm_rope(z, gain_ref, out_refs, scale):
        for j in range(ATTN_WIDTH // 256):
            zj = z[:, 256 * j:256 * (j + 1)]
            ss = _dot((zj * zj).astype(BF16), bd_ref[...])
            zn = zj * lax.rsqrt(ss * (1.0 / HEAD_DIM) + EPS) * gain_ref[:, 256 * j:256 * (j + 1)]
            for t in range(2):
                zt = zn[:, LANES * t:LANES * (t + 1)]
                partner = jnp.where(upper, pltpu.roll(zt, HEAD_DIM // 2, 1),
                                    pltpu.roll(zt, LANES - HEAD_DIM // 2, 1))
                r = zt * cos + partner * sin
                c0 = 256 * j + LANES * t
                emit(r * scale, c0, out_refs[0], out_refs[1])

    a = ATTN_WIDTH
    d_hi, d_lo = DILATIONS[0], DILATIONS[1]
    head_norm_rope(proj[:, 0:a], qg_ref, (q_ref, ((d_hi, q16_ref), (d_lo, q4_ref))), HEAD_DIM ** -0.5)
    head_norm_rope(proj[:, a:2 * a], kg_ref, (k_ref, ((d_hi, k16_ref), (d_lo, k4_ref))), 1.0)
    for c0 in range(0, a, LANES):
        emit(proj[:, 2 * a + c0:2 * a + c0 + LANES], c0, v_ref, ((d_hi, v16_ref), (d_lo, v4_ref)))

    c = CONV_WIDTH
    bg = proj[:, 3 * a:3 * a + c]
    cg = proj[:, 3 * a + c:3 * a + 2 * c]
    xv = proj[:, 3 * a + 2 * c:3 * a + 3 * c]
    u = cg * xv

    @pl.when(i == 0)
    def _():
        ubuf_ref[0:SUBLANES, :] = jnp.zeros((SUBLANES, c), F32)

    ubuf_ref[SUBLANES:SUBLANES + tm, :] = u
    u1 = ubuf_ref[SUBLANES - 1:SUBLANES - 1 + tm, :]
    u2 = ubuf_ref[SUBLANES - 2:SUBLANES - 2 + tm, :]
    y = u2 * cw_ref[0:1, :] + u1 * cw_ref[1:2, :] + u * cw_ref[2:3, :]
    conv_ref[...] = (bg * y).astype(BF16)
    ubuf_ref[0:SUBLANES, :] = ubuf_ref[tm:tm + SUBLANES, :]


def _inproj(x, g, sc, sh, w_in, qg, kg, cos_t, sin_t, bd, conv_w, tm):
    s = x.shape[0]
    row = lambda n: pl.BlockSpec((1, n), lambda i: (0, 0))
    tok = lambda n: pl.BlockSpec((tm, n), lambda i: (i, 0))
    return pl.pallas_call(
        functools.partial(_inproj_kernel, tm=tm),
        grid=(s // tm,),
        in_specs=[tok(D_MODEL), row(D_MODEL), row(D_MODEL), row(D_MODEL),
                  pl.BlockSpec((D_MODEL, IN_PROJ_WIDTH), lambda i: (0, 0)),
                  row(ATTN_WIDTH), row(ATTN_WIDTH), tok(LANES), tok(LANES),
                  pl.BlockSpec((256, 256), lambda i: (0, 0)),
                  pl.BlockSpec((3, CONV_WIDTH), lambda i: (0, 0))],
        out_specs=[tok(ATTN_WIDTH)] * 3
        + [pl.BlockSpec((tm // d, d * ATTN_WIDTH), lambda i: (i, 0)) for d in DILATIONS[:2] for _ in range(3)]
        + [tok(CONV_WIDTH)],
        out_shape=[jax.ShapeDtypeStruct((s, ATTN_WIDTH), BF16)] * 3
        + [jax.ShapeDtypeStruct((s // d, d * ATTN_WIDTH), BF16) for d in DILATIONS[:2] for _ in range(3)]
        + [jax.ShapeDtypeStruct((s, CONV_WIDTH), BF16)],
        scratch_shapes=[pltpu.VMEM((tm + SUBLANES, CONV_WIDTH), F32),
                        pltpu.VMEM((3 * ATTN_WIDTH // LANES, tm, LANES), F32)],
        compiler_params=_params("arbitrary"),
        name="inproj",
    )(x, g, sc, sh, w_in, qg, kg, cos_t, sin_t, bd, conv_w)


def _attn_kernel(*refs, first, last):
    tq = BRANCH_KEYS
    if first:
        q_ref, kp_ref, kc_ref, vp_ref, vc_ref, acc_out, ml_out = refs
    elif last:
        q_ref, kp_ref, kc_ref, vp_ref, vc_ref, acc_in, ml_in, out_ref = refs
    else:
        q_ref, kp_ref, kc_ref, vp_ref, vc_ref, acc_in, ml_in, acc_out, ml_out = refs
    ib = pl.program_id(1)
    qi = lax.broadcasted_iota(jnp.int32, (tq, tq), 0)
    ki = lax.broadcasted_iota(jnp.int32, (tq, tq), 1)
    mask_c = ki <= qi
    mask_p = jnp.logical_and(ki >= qi, ib > 0)
    lane = ki
    ml_old = None if first else ml_in[...]
    heads = range(N_ATTN_HEADS)
    sls = [slice(HEAD_DIM * h, HEAD_DIM * (h + 1)) for h in heads]
    sp = [jnp.where(mask_p, _dot_nt(q_ref[:, sl], kp_ref[:, sl]), NEG_INF) for sl in sls]
    sc = [jnp.where(mask_c, _dot_nt(q_ref[:, sl], kc_ref[:, sl]), NEG_INF) for sl in sls]
    m_new = [jnp.max(jnp.maximum(a, b), axis=1, keepdims=True) for a, b in zip(sp, sc)]
    if not first:
        m_old = [jnp.max(jnp.where(lane == h, ml_old, NEG_INF), axis=1, keepdims=True) for h in heads]
        l_old = [jnp.sum(jnp.where(lane == 16 + h, ml_old, 0.0), axis=1, keepdims=True) for h in heads]
        m_new = [jnp.maximum(a, b) for a, b in zip(m_old, m_new)]
        alpha = [jnp.exp(a - b) for a, b in zip(m_old, m_new)]
    pp = [jnp.exp(s - m) for s, m in zip(sp, m_new)]
    pc = [jnp.exp(s - m) for s, m in zip(sc, m_new)]
    l_new = [jnp.sum(a + b, axis=1, keepdims=True) for a, b in zip(pp, pc)]
    acc = [_dot(a.astype(BF16), vp_ref[:, sl]) + _dot(b.astype(BF16), vc_ref[:, sl])
           for a, b, sl in zip(pp, pc, sls)]
    if not first:
        l_new = [a * lo + ln for a, lo, ln in zip(alpha, l_old, l_new)]
        acc = [a * acc_in[:, sl].astype(F32) + pv for a, sl, pv in zip(alpha, sls, acc)]
    if last:
        for sl, a, l in zip(sls, acc, l_new):
            out_ref[:, sl] = (a / l).astype(BF16)
    else:
        ml_new = jnp.zeros((tq, LANES), F32)
        for h, sl in zip(heads, sls):
            acc_out[:, sl] = acc[h].astype(BF16)
            ml_new = jnp.where(lane == h, m_new[h], ml_new)
            ml_new = jnp.where(lane == 16 + h, l_new[h], ml_new)
        ml_out[...] = ml_new


def _attn_pass(q, k, v, state, dil, first, last):
    n = q.shape[0]
    s = n * dil
    tq = BRANCH_KEYS
    view = lambda a: a.reshape(n, dil * a.shape[1])
    cur = lambda w: pl.BlockSpec((tq, w), lambda r, ib: (ib, r))
    prev = lambda w: pl.BlockSpec((tq, w), lambda r, ib: (jnp.maximum(ib - 1, 0), r))
    aw = ATTN_WIDTH
    in_specs = [cur(aw), prev(aw), cur(aw), prev(aw), cur(aw)]
    args = [q, k, k, v, v]
    if not first:
        in_specs += [cur(aw), cur(LANES)]
        args += [view(state[0]), view(state[1])]
    if last:
        out_specs = [cur(aw)]
        out_shape = [jax.ShapeDtypeStruct((n, dil * aw), BF16)]
    else:
        out_specs = [cur(aw), cur(LANES)]
        out_shape = [jax.ShapeDtypeStruct((n, dil * aw), BF16), jax.ShapeDtypeStruct((n, dil * LANES), F32)]
    outs = pl.pallas_call(
        functools.partial(_attn_kernel, first=first, last=last),
        grid=(dil, n // tq),
        in_specs=in_specs, out_specs=out_specs, out_shape=out_shape,
        compiler_params=_params("parallel", "parallel"),
        name=f"attn_d{dil}",
    )(*args)
    return [o.reshape(s, -1) for o in outs]


def _attention(qkv_views):
    state = None
    for idx, dil in enumerate(DILATIONS):
        state = _attn_pass(*qkv_views[dil], state, dil, first=idx == 0, last=idx == len(DILATIONS) - 1)
    return state[0]


def _outproj_kernel(x_ref, attn_ref, conv_ref, woa_ref, woc_ref, gt_ref, g_ref, sc_ref, sh_ref, wq_ref,
                    keys_ref, x1_ref, h2_ref, st_ref):
    mix = _dot(attn_ref[...], woa_ref[...]) + _dot(conv_ref[...], woc_ref[...])
    x1 = x_ref[...] + gt_ref[...] * mix
    x1_ref[...] = x1
    h2 = _rms_mod(x1, g_ref[...], sc_ref[...], sh_ref[...]).astype(BF16)
    h2_ref[...] = h2
    qp = _dot(h2, wq_ref[...])
    for g in range(2 * PEER_HEADS):
        qg = qp[:, PEER_NKEYS * g:PEER_NKEYS * (g + 1)].astype(BF16)
        st_ref[PEER_NKEYS * g:PEER_NKEYS * (g + 1), :] = _dot_nt(keys_ref[g], qg)


def _outproj(x, attn, conv, wo_a, wo_c, gt, g, sc, sh, wq, keys, tm):
    s = x.shape[0]
    nq = wq.shape[1]
    row = lambda n: pl.BlockSpec((1, n), lambda i: (0, 0))
    tok = lambda n: pl.BlockSpec((tm, n), lambda i: (i, 0))
    full = lambda a: pl.BlockSpec(a.shape, lambda i: (0,) * a.ndim)
    return pl.pallas_call(
        _outproj_kernel,
        grid=(s // tm,),
        in_specs=[tok(D_MODEL), tok(ATTN_WIDTH), tok(CONV_WIDTH), full(wo_a), full(wo_c),
                  row(D_MODEL), row(D_MODEL), row(D_MODEL), row(D_MODEL), full(wq), full(keys)],
        out_specs=[tok(D_MODEL), tok(D_MODEL), pl.BlockSpec((nq, tm), lambda i: (0, i))],
        out_shape=[jax.ShapeDtypeStruct((s, D_MODEL), F32), jax.ShapeDtypeStruct((s, D_MODEL), BF16),
                   jax.ShapeDtypeStruct((nq, s), F32)],
        compiler_params=_params("parallel"),
        name="outproj",
    )(x, attn, conv, wo_a, wo_c, gt, g, sc, sh, wq, keys)


def _top16(ss, tb):
    n = ss[0].shape[0]
    row = lax.broadcasted_iota(jnp.int32, (n, tb), 0)
    row16 = lax.broadcasted_iota(jnp.int32, (PEER_TOPK, tb), 0)
    cur = list(ss)
    vals = [jnp.zeros((PEER_TOPK, tb), F32) for _ in ss]
    mark = lambda kk: -(F32_MAX - kk * F32_MAX_ULP)
    for kk in range(PEER_TOPK):
        m = [jnp.max(c, axis=0, keepdims=True) for c in cur]
        idx = [jnp.min(jnp.where(c == mm, row, n), axis=0, keepdims=True) for c, mm in zip(cur, m)]
        cur = [jnp.where(row == i, mark(kk), c) for i, c in zip(idx, cur)]
        vals = [jnp.where(row16 == kk, mm, v) for mm, v in zip(m, vals)]
    rank = [jnp.where(c <= mark(PEER_TOPK - 1), (c - mark(0)) * (1.0 / F32_MAX_ULP), float(PEER_TOPK))
            for c in cur]
    return vals, rank


def _topk_kernel(st_ref, rank2_ref, cnt_ref, e1_ref, e2_ref, *, tb):
    kk = PEER_TOPK
    s1 = st_ref[0:PEER_NKEYS, :]
    s2 = st_ref[PEER_NKEYS:2 * PEER_NKEYS, :]
    (v1, v2), (rank1, rank2) = _top16([s1, s2], tb)

    sub = lax.broadcasted_iota(jnp.int32, (SUBLANES, tb), 0)
    v2lo, v2hi = v2[0:SUBLANES, :], v2[SUBLANES:kk, :]
    groups = [(v1[0:1, :] + v2lo, sub, 0), (v1[0:1, :] + v2hi, sub + SUBLANES, 0),
              (v1[1:2, :] + v2lo, sub + kk, 1)]
    for a in range(2, SUBLANES):
        nb = kk // (a + 1)
        groups.append((jnp.where(sub < nb, v1[a:a + 1, :] + v2lo, NEG_INF), sub + kk * a, a))
    groups.append((v1[SUBLANES:kk, :] + v2[0:1, :], (sub + SUBLANES) * kk, None))
    cand = [g[0] for g in groups]
    flat = [g[1] for g in groups]
    cur = list(cand)
    big = kk * kk
    for _ in range(kk):
        m = functools.reduce(jnp.maximum, cur)
        m = jnp.max(m, axis=0, keepdims=True)
        hit = [jnp.where(c == m, f, big) for c, f in zip(cur, flat)]
        idx = jnp.min(functools.reduce(jnp.minimum, hit), axis=0, keepdims=True)
        cur = [jnp.where(f == idx, NEG_INF, c) for c, f in zip(cur, flat)]
    sel = [jnp.logical_and(c == NEG_INF, o != NEG_INF) for c, o in zip(cur, cand)]

    one = lambda mk: jnp.where(mk, 1.0, 0.0)
    counts = [jnp.sum(one(sel[0]) + one(sel[1]), axis=0, keepdims=True)]
    for gi in range(2, 2 + SUBLANES - 1):
        counts.append(jnp.sum(one(sel[gi]), axis=0, keepdims=True))
    tail = one(sel[-1])
    cnt = jnp.zeros((PEER_NKEYS, tb), F32)
    for a in range(kk):
        na = counts[a] if a < SUBLANES else tail[a - SUBLANES:a - SUBLANES + 1, :]
        cnt = jnp.where(rank1 == float(a), na, cnt)

    m1, m2 = v1[0:1, :], v2[0:1, :]
    mt = m1 + m2
    z = functools.reduce(lambda x, y: x + y,
                         [jnp.sum(jnp.where(sl, jnp.exp(o - mt), 0.0), axis=0, keepdims=True)
                          for sl, o in zip(sel, cand)])
    rank2_ref[...] = pltpu.bitcast(rank2.astype(BF16), jnp.uint32)
    cnt_ref[...] = cnt
    e1_ref[...] = jnp.exp(s1 - m1)
    e2_ref[...] = pltpu.bitcast((jnp.exp(s2 - m2) / z).astype(BF16), jnp.uint32)


def _topk(st, tb):
    s = st.shape[1]
    crow = PEER_NKEYS * SUBLANES // PACK
    plane = pl.BlockSpec((None, PEER_NKEYS, tb), lambda h, i: (h, 0, i))
    cplane = pl.BlockSpec((None, crow, tb), lambda h, i: (h, 0, i))
    shape = jax.ShapeDtypeStruct((PEER_HEADS, PEER_NKEYS, s), F32)
    cshape = jax.ShapeDtypeStruct((PEER_HEADS, crow, s), jnp.uint32)
    return pl.pallas_call(
        functools.partial(_topk_kernel, tb=tb),
        grid=(PEER_HEADS, s // tb),
        in_specs=[pl.BlockSpec((2 * PEER_NKEYS, tb), lambda h, i: (h, i))],
        out_specs=[cplane, plane, plane, cplane],
        out_shape=[cshape, shape, shape, cshape],
        compiler_params=_params("parallel", "parallel"),
        name="peer_topk",
    )(st)


PACK = 16
PEER_PIECES = 4


def _peer_kernel(h_ref, u_ref, vt_ref, rank2_ref, cnt_ref, e1_ref, e2_ref, x1_ref, gt_ref,
                 out_ref, acc_ref, p_ref, *, tm, ec):
    j = pl.program_id(1)
    rows = ec // PEER_NKEYS

    @pl.when(j == 0)
    def _():
        acc_ref[...] = jnp.zeros_like(acc_ref)

    crows = pl.ds(pl.multiple_of(j * rows, SUBLANES), rows)
    nt = tm // LANES
    tiles_per_piece = nt // PEER_PIECES
    er = ec // PEER_PIECES
    erw = er * SUBLANES // PACK
    zt = []
    for t in range(nt):
        if t % tiles_per_piece == 0:
            q = t // tiles_per_piece
            u_q = pltpu.bitcast(u_ref[erw * q:erw * (q + 1), :], BF16)
            zt.append(_dot_nt(u_q, h_ref[...]))
        tl = slice(LANES * t, LANES * (t + 1))
        cnt_rows = [cnt_ref[hh, crows, tl] for hh in range(PEER_HEADS)]
        e1_rows = [e1_ref[hh, crows, tl] for hh in range(PEER_HEADS)]
        for r in range(rows):
            w = [None] * (PEER_NKEYS // PACK)
            for hh in range(PEER_HEADS):
                cb = jnp.broadcast_to(cnt_rows[hh][r:r + 1, :], (PACK, LANES)).astype(BF16)
                eb = jnp.broadcast_to(e1_rows[hh][r:r + 1, :], (PACK, LANES)).astype(BF16)
                for g in range(PEER_NKEYS // PACK):
                    gs = slice(SUBLANES * g, SUBLANES * (g + 1))
                    rk = pltpu.bitcast(rank2_ref[hh, gs, tl], BF16)
                    e2 = pltpu.bitcast(e2_ref[hh, gs, tl], BF16)
                    term = jnp.where(rk < cb, e2, 0.0) * eb
                    w[g] = term if w[g] is None else w[g] + term
            p_ref[PEER_NKEYS * r:PEER_NKEYS * (r + 1), tl] = jnp.concatenate(w, axis=0)
    for q in range(PEER_PIECES):
        rs = slice(er * q, er * (q + 1))
        p_ref[rs, :] = p_ref[rs, :] * _gelu(zt[q].astype(BF16))
    acc_ref[...] += _dot(pltpu.bitcast(vt_ref[...], BF16), p_ref[...])

    @pl.when(j == pl.num_programs(1) - 1)
    def _():
        out_ref[...] = x1_ref[...] + gt_ref[...] * acc_ref[...].T


def _peer(h2, u, vt, planes, x1, gt, tm, ec):
    s = h2.shape[0]
    ne = vt.shape[1]
    assert ec % (SUBLANES * PEER_NKEYS) == 0 and (tm // LANES) % PEER_PIECES == 0, (ec, tm)
    plane = pl.BlockSpec((PEER_HEADS, PEER_NKEYS, tm), lambda i, j: (0, 0, i))
    cplane = pl.BlockSpec((PEER_HEADS, PEER_NKEYS * SUBLANES // PACK, tm), lambda i, j: (0, 0, i))
    tok = pl.BlockSpec((tm, D_MODEL), lambda i, j: (i, 0))
    return pl.pallas_call(
        functools.partial(_peer_kernel, tm=tm, ec=ec),
        grid=(s // tm, ne // ec),
        in_specs=[tok, pl.BlockSpec((ec * SUBLANES // PACK, D_MODEL), lambda i, j: (j, 0)),
                  pl.BlockSpec((D_MODEL * SUBLANES // PACK, ec), lambda i, j: (0, j)),
                  cplane, plane, plane, cplane, tok, pl.BlockSpec((1, D_MODEL), lambda i, j: (0, 0))],
        out_specs=tok,
        out_shape=jax.ShapeDtypeStruct((s, D_MODEL), F32),
        scratch_shapes=[pltpu.VMEM((D_MODEL, tm), F32), pltpu.VMEM((ec, tm), BF16)],
        compiler_params=_params("parallel", "arbitrary"),
        name="peer_dense",
    )(h2, u, vt, *planes, x1, gt)


def _words_kernel(x_ref, o_ref, *, transpose):
    x = x_ref[...]
    if transpose:
        x = x.T
    o_ref[...] = pltpu.bitcast(x.astype(BF16), jnp.uint32)


def _expert_table_words(tabs, layer, transpose):
    _, ne, d = tabs.shape
    blk = D_MODEL
    shrink = PACK // SUBLANES
    if transpose:
        out_shape, out_spec = (d // shrink, ne), pl.BlockSpec((d // shrink, blk), lambda i: (0, i))
    else:
        out_shape, out_spec = (ne // shrink, d), pl.BlockSpec((blk // shrink, d), lambda i: (i, 0))
    return pl.pallas_call(
        functools.partial(_words_kernel, transpose=transpose),
        grid=(ne // blk,),
        in_specs=[pl.BlockSpec((None, blk, d), lambda i: (layer, i, 0))],
        out_specs=out_spec,
        out_shape=jax.ShapeDtypeStruct(out_shape, jnp.uint32),
        compiler_params=_params("parallel"),
        name="expert_words_t" if transpose else "expert_words",
    )(tabs)


def _tile(s, want):
    t = min(want, s)
    assert s % t == 0, (s, t)
    return t


def _layer(x, mod, cos_t, sin_t, bd, norm_mix, norm_ffn, w_in, q_norm, k_norm, conv_w, w_out,
           peer_wq, peer_keys, peer_u_all, peer_v_all, layer):
    s = x.shape[0]
    sh1, sc1, gt1, sh2, sc2, gt2 = [m.reshape(1, D_MODEL) for m in jnp.split(mod, N_MOD)]
    tm = _tile(s, 512)
    qg = jnp.tile(q_norm, N_ATTN_HEADS).reshape(1, ATTN_WIDTH)
    kg = jnp.tile(k_norm, N_ATTN_HEADS).reshape(1, ATTN_WIDTH)
    *qkv, conv = _inproj(x, norm_mix.reshape(1, D_MODEL), sc1, sh1, w_in.astype(BF16), qg, kg,
                         cos_t, sin_t, bd, conv_w, tm)
    views = {1: qkv[0:3], DILATIONS[0]: qkv[3:6], DILATIONS[1]: qkv[6:9]}
    attn = _attention(views)
    wo = w_out.astype(BF16)
    keys = peer_keys.reshape(2 * PEER_HEADS, PEER_NKEYS, -1).astype(BF16)
    x1, h2, st = _outproj(x, attn, conv, wo[:ATTN_WIDTH], wo[ATTN_WIDTH:], gt1,
                          norm_ffn.reshape(1, D_MODEL), sc2, sh2, peer_wq.astype(BF16), keys, tm)
    planes = _topk(st, _tile(s, 1024))
    return _peer(h2, _expert_table_words(peer_u_all, layer, False), _expert_table_words(peer_v_all, layer, True),
                 planes, x1, gt2, _tile(s, 512), 2 * SUBLANES * PEER_NKEYS)


def kernel(x, c, positions, w_ada, b_ada, norm_mix, norm_ffn, w_in, q_norm, k_norm, conv_w, w_out,
           peer_wq, peer_keys, peer_u, peer_v):
    b, s, d = x.shape
    assert b == 1 and d == D_MODEL and s % (max(DILATIONS) * BRANCH_KEYS) == 0, x.shape
    depth = w_ada.shape[0]
    mod = _ada(c, w_ada, b_ada)
    cos_t, sin_t = _rope_tables(positions, _tile(s, 1024))
    blk = jnp.arange(256) // HEAD_DIM
    bd = (blk[:, None] == blk[None, :]).astype(BF16)
    y = x.reshape(s, d)
    for l in range(depth):
        y = _layer(y, mod[l], cos_t, sin_t, bd, norm_mix[l], norm_ffn[l], w_in[l], q_norm[l], k_norm[l],
                   conv_w[l], w_out[l], peer_wq[l], peer_keys[l], peer_u, peer_v, l)
    return y.reshape(b, s, d)
```

```python
import functools

import jax
import jax.numpy as jnp
from jax import lax
from jax.experimental import pallas as pl
from jax.experimental.pallas import tpu as pltpu

F32 = jnp.float32
BF16 = jnp.bfloat16

D_MODEL = 1024
HEAD_DIM = 64
N_ATTN_HEADS = 12
ATTN_WIDTH = N_ATTN_HEADS * HEAD_DIM
CONV_WIDTH = D_MODEL - ATTN_WIDTH
IN_PROJ_WIDTH = 3 * ATTN_WIDTH + 3 * CONV_WIDTH
BRANCH_KEYS = 128
DILATIONS = (16, 4, 1)
ROPE_THETA = 10000.0
EPS = 1e-6
PEER_HEADS = 8
PEER_NKEYS = 128
PEER_TOPK = 16
N_MOD = 6

LANES = 128
SUBLANES = 8
VMEM_LIMIT = 56 * 1024 * 1024

NEG_INF = float("-inf")
F32_MAX = float(jnp.finfo(jnp.float32).max)
F32_MAX_ULP = 2.0 ** 104


def _dot(a, b):
    return jnp.dot(a, b, preferred_element_type=F32)


def _dot_nt(a, b):
    return lax.dot_general(a, b, (((1,), (1,)), ((), ())), preferred_element_type=F32)


def _gelu(x):
    return 0.5 * x * (1.0 + lax.erf(x * (0.5 ** 0.5)))


def _params(*sem):
    return pltpu.CompilerParams(dimension_semantics=sem, vmem_limit_bytes=VMEM_LIMIT)


def _ada_kernel(c_ref, w_ref, b_ref, o_ref):
    o_ref[...] = jnp.dot(c_ref[...], w_ref[...], preferred_element_type=F32,
                         precision=lax.Precision.HIGHEST) + b_ref[...]


def _ada(c, w_ada, b_ada):
    depth = w_ada.shape[0]
    c8 = jnp.broadcast_to(c, (SUBLANES, D_MODEL))
    out = pl.pallas_call(
        _ada_kernel,
        grid=(depth, N_MOD),
        in_specs=[pl.BlockSpec((SUBLANES, D_MODEL), lambda l, j: (0, 0)),
                  pl.BlockSpec((None, D_MODEL, D_MODEL), lambda l, j: (l, 0, j)),
                  pl.BlockSpec((None, 1, D_MODEL), lambda l, j: (l, 0, j))],
        out_specs=pl.BlockSpec((None, SUBLANES, D_MODEL), lambda l, j: (l, 0, j)),
        out_shape=jax.ShapeDtypeStruct((depth, SUBLANES, N_MOD * D_MODEL), F32),
        compiler_params=_params("parallel", "parallel"),
        name="ada_mod",
    )(c8, w_ada, b_ada.reshape(depth, 1, N_MOD * D_MODEL))
    return out[:, 0, :]


def _rope_kernel(pos_ref, freq_ref, sign_ref, cos_ref, sin_ref):
    ang = pos_ref[...].astype(F32) * freq_ref[...]
    cos_ref[...] = jnp.cos(ang)
    sin_ref[...] = jnp.sin(ang) * sign_ref[...]


def _rope_tables(positions, tm):
    s = positions.shape[-1]
    half = HEAD_DIM // 2
    freq = ROPE_THETA ** (-jnp.arange(half, dtype=F32) / half)
    freq = jnp.tile(freq, LANES // half).reshape(1, LANES)
    sign = jnp.tile(jnp.concatenate([-jnp.ones((half,), F32), jnp.ones((half,), F32)]), LANES // HEAD_DIM)
    sign = sign.reshape(1, LANES)
    row = pl.BlockSpec((1, LANES), lambda i: (0, 0))
    return pl.pallas_call(
        _rope_kernel,
        grid=(s // tm,),
        in_specs=[pl.BlockSpec((tm, 1), lambda i: (i, 0)), row, row],
        out_specs=[pl.BlockSpec((tm, LANES), lambda i: (i, 0))] * 2,
        out_shape=[jax.ShapeDtypeStruct((s, LANES), F32)] * 2,
        compiler_params=_params("parallel"),
        name="rope_tables",
    )(positions.reshape(s, 1), freq, sign)


def _rms_mod(x, g, sc, sh):
    ms = jnp.mean(x * x, axis=-1, keepdims=True)
    y = x * lax.rsqrt(ms + EPS) * g
    return y * (1.0 + sc) + sh


def _inproj_kernel(x_ref, g_ref, sc_ref, sh_ref, w_ref, qg_ref, kg_ref, cos_ref, sin_ref, bd_ref, cw_ref,
                   q_ref, k_ref, v_ref, q16_ref, k16_ref, v16_ref, q4_ref, k4_ref, v4_ref, conv_ref,
                   ubuf_ref, stage_ref, *, tm):
    i = pl.program_id(0)
    stage_slot = [0]

    def emit(val, c0, nat_ref, strided_refs):
        nat_ref[:, c0:c0 + LANES] = val.astype(BF16)
        slot = stage_slot[0]
        stage_slot[0] += 1
        stage_ref[slot] = val
        for dil, ref in strided_refs:
            for r in range(dil):
                piece = stage_ref[slot, pl.ds(r, tm // dil, stride=dil), :]
                ref[:, r * ATTN_WIDTH + c0:r * ATTN_WIDTH + c0 + LANES] = piece.astype(BF16)
    h = _rms_mod(x_ref[...], g_ref[...], sc_ref[...], sh_ref[...])
    proj = _dot(h.astype(BF16), w_ref[...])
    cos = cos_ref[...]
    sin = sin_ref[...]
    lane = lax.broadcasted_iota(jnp.int32, (tm, LANES), 1)
    upper = (lane & (HEAD_DIM // 2)) != 0

    def head_norm_rope(z, gain_ref, out_refs, scale):
        for j in range(ATTN_WIDTH // 256):
            zj = z[:, 256 * j:256 * (j + 1)]
            ss = _dot((zj * zj).astype(BF16), bd_ref[...])
            zn = zj * lax.rsqrt(ss * (1.0 / HEAD_DIM) + EPS) * gain_ref[:, 256 * j:256 * (j + 1)]
            for t in range(2):
                zt = zn[:, LANES * t:LANES * (t + 1)]
                partner = jnp.where(upper, pltpu.roll(zt, HEAD_DIM // 2, 1),
                                    pltpu.roll(zt, LANES - HEAD_DIM // 2, 1))
                r = zt * cos + partner * sin
                c0 = 256 * j + LANES * t
                emit(r * scale, c0, out_refs[0], out_refs[1])

    a = ATTN_WIDTH
    d_hi, d_lo = DILATIONS[0], DILATIONS[1]
    head_norm_rope(proj[:, 0:a], qg_ref, (q_ref, ((d_hi, q16_ref), (d_lo, q4_ref))), HEAD_DIM ** -0.5)
    head_norm_rope(proj[:, a:2 * a], kg_ref, (k_ref, ((d_hi, k16_ref), (d_lo, k4_ref))), 1.0)
    for c0 in range(0, a, LANES):
        emit(proj[:, 2 * a + c0:2 * a + c0 + LANES], c0, v_ref, ((d_hi, v16_ref), (d_lo, v4_ref)))

    c = CONV_WIDTH
    bg = proj[:, 3 * a:3 * a + c]
    cg = proj[:, 3 * a + c:3 * a + 2 * c]
    xv = proj[:, 3 * a + 2 * c:3 * a + 3 * c]
    u = cg * xv

    @pl.when(i == 0)
    def _():
        ubuf_ref[0:SUBLANES, :] = jnp.zeros((SUBLANES, c), F32)

    ubuf_ref[SUBLANES:SUBLANES + tm, :] = u
    u1 = ubuf_ref[SUBLANES - 1:SUBLANES - 1 + tm, :]
    u2 = ubuf_ref[SUBLANES - 2:SUBLANES - 2 + tm, :]
    y = u2 * cw_ref[0:1, :] + u1 * cw_ref[1:2, :] + u * cw_ref[2:3, :]
    conv_ref[...] = (bg * y).astype(BF16)
    ubuf_ref[0:SUBLANES, :] = ubuf_ref[tm:tm + SUBLANES, :]


def _inproj(x, g, sc, sh, w_in, qg, kg, cos_t, sin_t, bd, conv_w, tm):
    s = x.shape[0]
    row = lambda n: pl.BlockSpec((1, n), lambda i: (0, 0))
    tok = lambda n: pl.BlockSpec((tm, n), lambda i: (i, 0))
    return pl.pallas_call(
        functools.partial(_inproj_kernel, tm=tm),
        grid=(s // tm,),
        in_specs=[tok(D_MODEL), row(D_MODEL), row(D_MODEL), row(D_MODEL),
                  pl.BlockSpec((D_MODEL, IN_PROJ_WIDTH), lambda i: (0, 0)),
                  row(ATTN_WIDTH), row(ATTN_WIDTH), tok(LANES), tok(LANES),
                  pl.BlockSpec((256, 256), lambda i: (0, 0)),
                  pl.BlockSpec((3, CONV_WIDTH), lambda i: (0, 0))],
        out_specs=[tok(ATTN_WIDTH)] * 3
        + [pl.BlockSpec((tm // d, d * ATTN_WIDTH), lambda i: (i, 0)) for d in DILATIONS[:2] for _ in range(3)]
        + [tok(CONV_WIDTH)],
        out_shape=[jax.ShapeDtypeStruct((s, ATTN_WIDTH), BF16)] * 3
        + [jax.ShapeDtypeStruct((s // d, d * ATTN_WIDTH), BF16) for d in DILATIONS[:2] for _ in range(3)]
        + [jax.ShapeDtypeStruct((s, CONV_WIDTH), BF16)],
        scratch_shapes=[pltpu.VMEM((tm + SUBLANES, CONV_WIDTH), F32),
                        pltpu.VMEM((3 * ATTN_WIDTH // LANES, tm, LANES), F32)],
        compiler_params=_params("arbitrary"),
        name="inproj",
    )(x, g, sc, sh, w_in, qg, kg, cos_t, sin_t, bd, conv_w)


ATTN_BLOCKS_PER_STEP = 2


def _attn_kernel(*refs, first, last):
    tq = BRANCH_KEYS
    if first:
        q_ref, kp_ref, kc_ref, vp_ref, vc_ref, acc_out, ml_out = refs
    elif last:
        q_ref, kp_ref, kc_ref, vp_ref, vc_ref, acc_in, ml_in, out_ref = refs
    else:
        q_ref, kp_ref, kc_ref, vp_ref, vc_ref, acc_in, ml_in, acc_out, ml_out = refs
    ib = pl.program_id(1)
    qi = lax.broadcasted_iota(jnp.int32, (tq, tq), 0)
    ki = lax.broadcasted_iota(jnp.int32, (tq, tq), 1)
    mask_c = ki <= qi
    mask_p = ki >= qi
    lane = ki
    probs = []
    for sub in range(ATTN_BLOCKS_PER_STEP):
        rows = slice(tq * sub, tq * (sub + 1))
        before = slice(tq * (sub - 1), tq * sub)
        for h in range(N_ATTN_HEADS):
            sl = slice(HEAD_DIM * h, HEAD_DIM * (h + 1))
            if sub == 0:
                probs.append((rows, sl, h, kp_ref[:, sl], vp_ref[:, sl], jnp.logical_and(mask_p, ib > 0)))
            else:
                probs.append((rows, sl, h, kc_ref[before, sl], vc_ref[before, sl], mask_p))
    sp = [jnp.where(mp, _dot_nt(q_ref[rows, sl], kprev), NEG_INF) for rows, sl, _, kprev, _, mp in probs]
    sc = [jnp.where(mask_c, _dot_nt(q_ref[rows, sl], kc_ref[rows, sl]), NEG_INF) for rows, sl, *_ in probs]
    m_new = [jnp.max(jnp.maximum(a, b), axis=1, keepdims=True) for a, b in zip(sp, sc)]
    if not first:
        ml_old = [ml_in[tq * sub:tq * (sub + 1), :] for sub in range(ATTN_BLOCKS_PER_STEP)]
        old = [ml_old[rows.start // tq] for rows, *_ in probs]
        m_old = [jnp.max(jnp.where(lane == p[2], o, NEG_INF), axis=1, keepdims=True) for p, o in zip(probs, old)]
        l_old = [jnp.sum(jnp.where(lane == 16 + p[2], o, 0.0), axis=1, keepdims=True) for p, o in zip(probs, old)]
        m_new = [jnp.maximum(a, b) for a, b in zip(m_old, m_new)]
        alpha = [jnp.exp(a - b) for a, b in zip(m_old, m_new)]
    pp = [jnp.exp(s - m) for s, m in zip(sp, m_new)]
    pc = [jnp.exp(s - m) for s, m in zip(sc, m_new)]
    l_new = [jnp.sum(a + b, axis=1, keepdims=True) for a, b in zip(pp, pc)]
    acc = [_dot(a.astype(BF16), p[4]) + _dot(b.astype(BF16), vc_ref[p[0], p[1]]) for a, b, p in zip(pp, pc, probs)]
    if not first:
        l_new = [a * lo + ln for a, lo, ln in zip(alpha, l_old, l_new)]
        acc = [a * acc_in[p[0], p[1]].astype(F32) + pv for a, p, pv in zip(alpha, probs, acc)]
    if last:
        for p, a, l in zip(probs, acc, l_new):
            out_ref[p[0], p[1]] = (a / l).astype(BF16)
    else:
        ml_new = [jnp.zeros((tq, LANES), F32) for _ in range(ATTN_BLOCKS_PER_STEP)]
        for p, a, m, l in zip(probs, acc, m_new, l_new):
            rows, sl, h = p[0], p[1], p[2]
            acc_out[rows, sl] = a.astype(BF16)
            sub = rows.start // tq
            ml_new[sub] = jnp.where(lane == h, m, ml_new[sub])
            ml_new[sub] = jnp.where(lane == 16 + h, l, ml_new[sub])
        for sub in range(ATTN_BLOCKS_PER_STEP):
            ml_out[tq * sub:tq * (sub + 1), :] = ml_new[sub]


def _attn_pass(q, k, v, state, dil, first, last):
    n = q.shape[0]
    s = n * dil
    tq = BRANCH_KEYS
    nb = ATTN_BLOCKS_PER_STEP
    view = lambda a: a.reshape(n, dil * a.shape[1])
    cur = lambda w: pl.BlockSpec((nb * tq, w), lambda r, ib: (ib, r))
    prev = lambda w: pl.BlockSpec((tq, w), lambda r, ib: (jnp.maximum(nb * ib - 1, 0), r))
    aw = ATTN_WIDTH
    in_specs = [cur(aw), prev(aw), cur(aw), prev(aw), cur(aw)]
    args = [q, k, k, v, v]
    if not first:
        in_specs += [cur(aw), cur(LANES)]
        args += [view(state[0]), view(state[1])]
    if last:
        out_specs = [cur(aw)]
        out_shape = [jax.ShapeDtypeStruct((n, dil * aw), BF16)]
    else:
        out_specs = [cur(aw), cur(LANES)]
        out_shape = [jax.ShapeDtypeStruct((n, dil * aw), BF16), jax.ShapeDtypeStruct((n, dil * LANES), F32)]
    outs = pl.pallas_call(
        functools.partial(_attn_kernel, first=first, last=last),
        grid=(dil, n // (nb * tq)),
        in_specs=in_specs, out_specs=out_specs, out_shape=out_shape,
        compiler_params=_params("parallel", "parallel"),
        name=f"attn_d{dil}",
    )(*args)
    return [o.reshape(s, -1) for o in outs]


def _attention(qkv_views):
    state = None
    for idx, dil in enumerate(DILATIONS):
        state = _attn_pass(*qkv_views[dil], state, dil, first=idx == 0, last=idx == len(DILATIONS) - 1)
    return state[0]


def _outproj_kernel(x_ref, attn_ref, conv_ref, woa_ref, woc_ref, gt_ref, g_ref, sc_ref, sh_ref, wq_ref,
                    keys_ref, x1_ref, h2_ref, st_ref):
    mix = _dot(attn_ref[...], woa_ref[...]) + _dot(conv_ref[...], woc_ref[...])
    x1 = x_ref[...] + gt_ref[...] * mix
    x1_ref[...] = x1
    h2 = _rms_mod(x1, g_ref[...], sc_ref[...], sh_ref[...]).astype(BF16)
    h2_ref[...] = h2
    qp = _dot(h2, wq_ref[...])
    for g in range(2 * PEER_HEADS):
        qg = qp[:, PEER_NKEYS * g:PEER_NKEYS * (g + 1)].astype(BF16)
        st_ref[PEER_NKEYS * g:PEER_NKEYS * (g + 1), :] = _dot_nt(keys_ref[g], qg)


def _outproj(x, attn, conv, wo_a, wo_c, gt, g, sc, sh, wq, keys, tm):
    s = x.shape[0]
    nq = wq.shape[1]
    row = lambda n: pl.BlockSpec((1, n), lambda i: (0, 0))
    tok = lambda n: pl.BlockSpec((tm, n), lambda i: (i, 0))
    full = lambda a: pl.BlockSpec(a.shape, lambda i: (0,) * a.ndim)
    return pl.pallas_call(
        _outproj_kernel,
        grid=(s // tm,),
        in_specs=[tok(D_MODEL), tok(ATTN_WIDTH), tok(CONV_WIDTH), full(wo_a), full(wo_c),
                  row(D_MODEL), row(D_MODEL), row(D_MODEL), row(D_MODEL), full(wq), full(keys)],
        out_specs=[tok(D_MODEL), tok(D_MODEL), pl.BlockSpec((nq, tm), lambda i: (0, i))],
        out_shape=[jax.ShapeDtypeStruct((s, D_MODEL), F32), jax.ShapeDtypeStruct((s, D_MODEL), BF16),
                   jax.ShapeDtypeStruct((nq, s), F32)],
        compiler_params=_params("parallel"),
        name="outproj",
    )(x, attn, conv, wo_a, wo_c, gt, g, sc, sh, wq, keys)


def _top16(ss, tb):
    n = ss[0].shape[0]
    row = lax.broadcasted_iota(jnp.int32, (n, tb), 0)
    row16 = lax.broadcasted_iota(jnp.int32, (PEER_TOPK, tb), 0)
    cur = list(ss)
    vals = [jnp.zeros((PEER_TOPK, tb), F32) for _ in ss]
    mark = lambda kk: -(F32_MAX - kk * F32_MAX_ULP)
    for kk in range(PEER_TOPK):
        m = [jnp.max(c, axis=0, keepdims=True) for c in cur]
        idx = [jnp.min(jnp.where(c == mm, row, n), axis=0, keepdims=True) for c, mm in zip(cur, m)]
        cur = [jnp.where(row == i, mark(kk), c) for i, c in zip(idx, cur)]
        vals = [jnp.where(row16 == kk, mm, v) for mm, v in zip(m, vals)]
    rank = [jnp.where(c <= mark(PEER_TOPK - 1), (c - mark(0)) * (1.0 / F32_MAX_ULP), float(PEER_TOPK))
            for c in cur]
    return vals, rank


def _topk_kernel(st_ref, rank2_ref, cnt_ref, e1_ref, e2_ref, *, tb):
    kk = PEER_TOPK
    s1 = st_ref[0:PEER_NKEYS, :]
    s2 = st_ref[PEER_NKEYS:2 * PEER_NKEYS, :]
    (v1, v2), (rank1, rank2) = _top16([s1, s2], tb)

    sub = lax.broadcasted_iota(jnp.int32, (SUBLANES, tb), 0)
    v2lo, v2hi = v2[0:SUBLANES, :], v2[SUBLANES:kk, :]
    groups = [(v1[0:1, :] + v2lo, sub, 0), (v1[0:1, :] + v2hi, sub + SUBLANES, 0),
              (v1[1:2, :] + v2lo, sub + kk, 1)]
    for a in range(2, SUBLANES):
        nb = kk // (a + 1)
        groups.append((jnp.where(sub < nb, v1[a:a + 1, :] + v2lo, NEG_INF), sub + kk * a, a))
    groups.append((v1[SUBLANES:kk, :] + v2[0:1, :], (sub + SUBLANES) * kk, None))
    cand = [g[0] for g in groups]
    flat = [g[1] for g in groups]
    cur = list(cand)
    big = kk * kk
    for _ in range(kk):
        m = functools.reduce(jnp.maximum, cur)
        m = jnp.max(m, axis=0, keepdims=True)
        hit = [jnp.where(c == m, f, big) for c, f in zip(cur, flat)]
        idx = jnp.min(functools.reduce(jnp.minimum, hit), axis=0, keepdims=True)
        cur = [jnp.where(f == idx, NEG_INF, c) for c, f in zip(cur, flat)]
    sel = [jnp.logical_and(c == NEG_INF, o != NEG_INF) for c, o in zip(cur, cand)]

    one = lambda mk: jnp.where(mk, 1.0, 0.0)
    counts = [jnp.sum(one(sel[0]) + one(sel[1]), axis=0, keepdims=True)]
    for gi in range(2, 2 + SUBLANES - 1):
        counts.append(jnp.sum(one(sel[gi]), axis=0, keepdims=True))
    tail = one(sel[-1])
    cnt = jnp.zeros((PEER_NKEYS, tb), F32)
    for a in range(kk):
        na = counts[a] if a < SUBLANES else tail[a - SUBLANES:a - SUBLANES + 1, :]
        cnt = jnp.where(rank1 == float(a), na, cnt)

    m1, m2 = v1[0:1, :], v2[0:1, :]
    mt = m1 + m2
    z = functools.reduce(lambda x, y: x + y,
                         [jnp.sum(jnp.where(sl, jnp.exp(o - mt), 0.0), axis=0, keepdims=True)
                          for sl, o in zip(sel, cand)])
    rank2_ref[...] = pltpu.bitcast(rank2.astype(BF16), jnp.uint32)
    cnt_ref[...] = cnt
    e1_ref[...] = jnp.exp(s1 - m1)
    e2_ref[...] = pltpu.bitcast((jnp.exp(s2 - m2) / z).astype(BF16), jnp.uint32)


def _topk(st, tb):
    s = st.shape[1]
    crow = PEER_NKEYS * SUBLANES // PACK
    plane = pl.BlockSpec((None, PEER_NKEYS, tb), lambda h, i: (h, 0, i))
    cplane = pl.BlockSpec((None, crow, tb), lambda h, i: (h, 0, i))
    shape = jax.ShapeDtypeStruct((PEER_HEADS, PEER_NKEYS, s), F32)
    cshape = jax.ShapeDtypeStruct((PEER_HEADS, crow, s), jnp.uint32)
    return pl.pallas_call(
        functools.partial(_topk_kernel, tb=tb),
        grid=(PEER_HEADS, s // tb),
        in_specs=[pl.BlockSpec((2 * PEER_NKEYS, tb), lambda h, i: (h, i))],
        out_specs=[cplane, plane, plane, cplane],
        out_shape=[cshape, shape, shape, cshape],
        compiler_params=_params("parallel", "parallel"),
        name="peer_topk",
    )(st)


PACK = 16
PEER_PIECES = 4


def _peer_kernel(h_ref, u_ref, vt_ref, rank2_ref, cnt_ref, e1_ref, e2_ref, x1_ref, gt_ref,
                 out_ref, acc_ref, p_ref, *, tm, ec):
    j = pl.program_id(1)
    rows = ec // PEER_NKEYS

    @pl.when(j == 0)
    def _():
        acc_ref[...] = jnp.zeros_like(acc_ref)

    crows = pl.ds(pl.multiple_of(j * rows, SUBLANES), rows)
    nt = tm // LANES
    tiles_per_piece = nt // PEER_PIECES
    er = ec // PEER_PIECES
    erw = er * SUBLANES // PACK
    zt = []
    for t in range(nt):
        if t % tiles_per_piece == 0:
            q = t // tiles_per_piece
            u_q = pltpu.bitcast(u_ref[erw * q:erw * (q + 1), :], BF16)
            zt.append(_dot_nt(u_q, h_ref[...]))
        tl = slice(LANES * t, LANES * (t + 1))
        cnt_rows = [cnt_ref[hh, crows, tl] for hh in range(PEER_HEADS)]
        e1_rows = [e1_ref[hh, crows, tl] for hh in range(PEER_HEADS)]
        for r in range(rows):
            w = [None] * (PEER_NKEYS // PACK)
            for hh in range(PEER_HEADS):
                cb = jnp.broadcast_to(cnt_rows[hh][r:r + 1, :], (PACK, LANES)).astype(BF16)
                eb = jnp.broadcast_to(e1_rows[hh][r:r + 1, :], (PACK, LANES)).astype(BF16)
                for g in range(PEER_NKEYS // PACK):
                    gs = slice(SUBLANES * g, SUBLANES * (g + 1))
                    rk = pltpu.bitcast(rank2_ref[hh, gs, tl], BF16)
                    e2 = pltpu.bitcast(e2_ref[hh, gs, tl], BF16)
                    term = jnp.where(rk < cb, e2, 0.0) * eb
                    w[g] = term if w[g] is None else w[g] + term
            p_ref[PEER_NKEYS * r:PEER_NKEYS * (r + 1), tl] = jnp.concatenate(w, axis=0)
    for q in range(PEER_PIECES):
        rs = slice(er * q, er * (q + 1))
        p_ref[rs, :] = p_ref[rs, :] * _gelu(zt[q].astype(BF16))
    acc_ref[...] += _dot(pltpu.bitcast(vt_ref[...], BF16), p_ref[...])

    @pl.when(j == pl.num_programs(1) - 1)
    def _():
        out_ref[...] = x1_ref[...] + gt_ref[...] * acc_ref[...].T


def _peer(h2, u, vt, planes, x1, gt, tm, ec):
    s = h2.shape[0]
    ne = vt.shape[1]
    assert ec % (SUBLANES * PEER_NKEYS) == 0 and (tm // LANES) % PEER_PIECES == 0, (ec, tm)
    plane = pl.BlockSpec((PEER_HEADS, PEER_NKEYS, tm), lambda i, j: (0, 0, i))
    cplane = pl.BlockSpec((PEER_HEADS, PEER_NKEYS * SUBLANES // PACK, tm), lambda i, j: (0, 0, i))
    tok = pl.BlockSpec((tm, D_MODEL), lambda i, j: (i, 0))
    return pl.pallas_call(
        functools.partial(_peer_kernel, tm=tm, ec=ec),
        grid=(s // tm, ne // ec),
        in_specs=[tok, pl.BlockSpec((ec * SUBLANES // PACK, D_MODEL), lambda i, j: (j, 0)),
                  pl.BlockSpec((D_MODEL * SUBLANES // PACK, ec), lambda i, j: (0, j)),
                  cplane, plane, plane, cplane, tok, pl.BlockSpec((1, D_MODEL), lambda i, j: (0, 0))],
        out_specs=tok,
        out_shape=jax.ShapeDtypeStruct((s, D_MODEL), F32),
        scratch_shapes=[pltpu.VMEM((D_MODEL, tm), F32), pltpu.VMEM((ec, tm), BF16)],
        compiler_params=_params("parallel", "arbitrary"),
        name="peer_dense",
    )(h2, u, vt, *planes, x1, gt)


def _words_kernel(x_ref, o_ref, *, transpose):
    x = x_ref[...]
    if transpose:
        x = x.T
    o_ref[...] = pltpu.bitcast(x.astype(BF16), jnp.uint32)


def _expert_table_words(tabs, layer, transpose):
    _, ne, d = tabs.shape
    blk = D_MODEL
    shrink = PACK // SUBLANES
    if transpose:
        out_shape, out_spec = (d // shrink, ne), pl.BlockSpec((d // shrink, blk), lambda i: (0, i))
    else:
        out_shape, out_spec = (ne // shrink, d), pl.BlockSpec((blk // shrink, d), lambda i: (i, 0))
    return pl.pallas_call(
        functools.partial(_words_kernel, transpose=transpose),
        grid=(ne // blk,),
        in_specs=[pl.BlockSpec((None, blk, d), lambda i: (layer, i, 0))],
        out_specs=out_spec,
        out_shape=jax.ShapeDtypeStruct(out_shape, jnp.uint32),
        compiler_params=_params("parallel"),
        name="expert_words_t" if transpose else "expert_words",
    )(tabs)


def _tile(s, want):
    t = min(want, s)
    assert s % t == 0, (s, t)
    return t


def _layer(x, mod, cos_t, sin_t, bd, norm_mix, norm_ffn, w_in, q_norm, k_norm, conv_w, w_out,
           peer_wq, peer_keys, peer_u_all, peer_v_all, layer):
    s = x.shape[0]
    sh1, sc1, gt1, sh2, sc2, gt2 = [m.reshape(1, D_MODEL) for m in jnp.split(mod, N_MOD)]
    tm = _tile(s, 512)
    qg = jnp.tile(q_norm, N_ATTN_HEADS).reshape(1, ATTN_WIDTH)
    kg = jnp.tile(k_norm, N_ATTN_HEADS).reshape(1, ATTN_WIDTH)
    *qkv, conv = _inproj(x, norm_mix.reshape(1, D_MODEL), sc1, sh1, w_in.astype(BF16), qg, kg,
                         cos_t, sin_t, bd, conv_w, tm)
    views = {1: qkv[0:3], DILATIONS[0]: qkv[3:6], DILATIONS[1]: qkv[6:9]}
    attn = _attention(views)
    wo = w_out.astype(BF16)
    keys = peer_keys.reshape(2 * PEER_HEADS, PEER_NKEYS, -1).astype(BF16)
    x1, h2, st = _outproj(x, attn, conv, wo[:ATTN_WIDTH], wo[ATTN_WIDTH:], gt1,
                          norm_ffn.reshape(1, D_MODEL), sc2, sh2, peer_wq.astype(BF16), keys, tm)
    planes = _topk(st, _tile(s, 1024))
    return _peer(h2, _expert_table_words(peer_u_all, layer, False), _expert_table_words(peer_v_all, layer, True),
                 planes, x1, gt2, _tile(s, 512), 2 * SUBLANES * PEER_NKEYS)


def kernel(x, c, positions, w_ada, b_ada, norm_mix, norm_ffn, w_in, q_norm, k_norm, conv_w, w_out,
           peer_wq, peer_keys, peer_u, peer_v):
    b, s, d = x.shape
    assert b == 1 and d == D_MODEL and s % (max(DILATIONS) * BRANCH_KEYS * ATTN_BLOCKS_PER_STEP) == 0, x.shape
    depth = w_ada.shape[0]
    mod = _ada(c, w_ada, b_ada)
    cos_t, sin_t = _rope_tables(positions, _tile(s, 1024))
    blk = jnp.arange(256) // HEAD_DIM
    bd = (blk[:, None] == blk[None, :]).astype(BF16)
    y = x.reshape(s, d)
    for l in range(depth):
        y = _layer(y, mod[l], cos_t, sin_t, bd, norm_mix[l], norm_ffn[l], w_in[l], q_norm[l], k_norm[l],
                   conv_w[l], w_out[l], peer_wq[l], peer_keys[l], peer_u, peer_v, l)
    return y.reshape(b, s, d)
```

```python
import functools

import jax
import jax.numpy as jnp
from jax import lax
from jax.experimental import pallas as pl
from jax.experimental.pallas import tpu as pltpu

F32 = jnp.float32
BF16 = jnp.bfloat16

D_MODEL = 1024
HEAD_DIM = 64
N_ATTN_HEADS = 12
ATTN_WIDTH = N_ATTN_HEADS * HEAD_DIM
CONV_WIDTH = D_MODEL - ATTN_WIDTH
IN_PROJ_WIDTH = 3 * ATTN_WIDTH + 3 * CONV_WIDTH
BRANCH_KEYS = 128
DILATIONS = (16, 4, 1)
ROPE_THETA = 10000.0
EPS = 1e-6
PEER_HEADS = 8
PEER_NKEYS = 128
PEER_TOPK = 16
N_MOD = 6

LANES = 128
SUBLANES = 8
VMEM_LIMIT = 56 * 1024 * 1024

NEG_INF = float("-inf")
F32_MAX = float(jnp.finfo(jnp.float32).max)
F32_MAX_ULP = 2.0 ** 104


def _dot(a, b):
    return jnp.dot(a, b, preferred_element_type=F32)


def _dot_nt(a, b):
    return lax.dot_general(a, b, (((1,), (1,)), ((), ())), preferred_element_type=F32)


def _gelu(x):
    return 0.5 * x * (1.0 + lax.erf(x * (0.5 ** 0.5)))


def _params(*sem):
    return pltpu.CompilerParams(dimension_semantics=sem, vmem_limit_bytes=VMEM_LIMIT)


def _ada_kernel(c_ref, w_ref, b_ref, o_ref):
    o_ref[...] = jnp.dot(c_ref[...], w_ref[...], preferred_element_type=F32,
                         precision=lax.Precision.HIGHEST) + b_ref[...]


def _ada(c, w_ada, b_ada):
    depth = w_ada.shape[0]
    c8 = jnp.broadcast_to(c, (SUBLANES, D_MODEL))
    out = pl.pallas_call(
        _ada_kernel,
        grid=(depth, N_MOD),
        in_specs=[pl.BlockSpec((SUBLANES, D_MODEL), lambda l, j: (0, 0)),
                  pl.BlockSpec((None, D_MODEL, D_MODEL), lambda l, j: (l, 0, j)),
                  pl.BlockSpec((None, 1, D_MODEL), lambda l, j: (l, 0, j))],
        out_specs=pl.BlockSpec((None, SUBLANES, D_MODEL), lambda l, j: (l, 0, j)),
        out_shape=jax.ShapeDtypeStruct((depth, SUBLANES, N_MOD * D_MODEL), F32),
        compiler_params=_params("parallel", "parallel"),
        name="ada_mod",
    )(c8, w_ada, b_ada.reshape(depth, 1, N_MOD * D_MODEL))
    return out[:, 0, :]


def _rope_kernel(pos_ref, freq_ref, sign_ref, cos_ref, sin_ref):
    ang = pos_ref[...].astype(F32) * freq_ref[...]
    cos_ref[...] = jnp.cos(ang)
    sin_ref[...] = jnp.sin(ang) * sign_ref[...]


def _rope_tables(positions, tm):
    s = positions.shape[-1]
    half = HEAD_DIM // 2
    freq = ROPE_THETA ** (-jnp.arange(half, dtype=F32) / half)
    freq = jnp.tile(freq, LANES // half).reshape(1, LANES)
    sign = jnp.tile(jnp.concatenate([-jnp.ones((half,), F32), jnp.ones((half,), F32)]), LANES // HEAD_DIM)
    sign = sign.reshape(1, LANES)
    row = pl.BlockSpec((1, LANES), lambda i: (0, 0))
    return pl.pallas_call(
        _rope_kernel,
        grid=(s // tm,),
        in_specs=[pl.BlockSpec((tm, 1), lambda i: (i, 0)), row, row],
        out_specs=[pl.BlockSpec((tm, LANES), lambda i: (i, 0))] * 2,
        out_shape=[jax.ShapeDtypeStruct((s, LANES), F32)] * 2,
        compiler_params=_params("parallel"),
        name="rope_tables",
    )(positions.reshape(s, 1), freq, sign)


def _rms_mod(x, g, sc, sh):
    ms = jnp.mean(x * x, axis=-1, keepdims=True)
    y = x * lax.rsqrt(ms + EPS) * g
    return y * (1.0 + sc) + sh


def _inproj_kernel(x_ref, g_ref, sc_ref, sh_ref, w_ref, qg_ref, kg_ref, cos_ref, sin_ref, bd_ref, cw_ref,
                   q_ref, k_ref, v_ref, q16_ref, k16_ref, v16_ref, q4_ref, k4_ref, v4_ref, conv_ref,
                   ubuf_ref, stage_ref, *, tm):
    i = pl.program_id(0)
    stage_slot = [0]

    def emit(val, c0, nat_ref, strided_refs):
        nat_ref[:, c0:c0 + LANES] = val.astype(BF16)
        slot = stage_slot[0]
        stage_slot[0] += 1
        stage_ref[slot] = val
        for dil, ref in strided_refs:
            for r in range(dil):
                piece = stage_ref[slot, pl.ds(r, tm // dil, stride=dil), :]
                ref[:, r * ATTN_WIDTH + c0:r * ATTN_WIDTH + c0 + LANES] = piece.astype(BF16)
    h = _rms_mod(x_ref[...], g_ref[...], sc_ref[...], sh_ref[...])
    proj = _dot(h.astype(BF16), w_ref[...])
    cos = cos_ref[...]
    sin = sin_ref[...]
    lane = lax.broadcasted_iota(jnp.int32, (tm, LANES), 1)
    from_first = pltpu.roll(lane, HEAD_DIM // 2, 1) == (lane ^ (HEAD_DIM // 2))

    def head_norm_rope(z, gain_ref, out_refs, scale):
        for j in range(ATTN_WIDTH // 256):
            zj = z[:, 256 * j:256 * (j + 1)]
            ss = _dot((zj * zj).astype(BF16), bd_ref[...])
            zn = zj * lax.rsqrt(ss * (1.0 / HEAD_DIM) + EPS) * gain_ref[:, 256 * j:256 * (j + 1)]
            for t in range(2):
                zt = zn[:, LANES * t:LANES * (t + 1)]
                partner = jnp.where(from_first, pltpu.roll(zt, HEAD_DIM // 2, 1),
                                    pltpu.roll(zt, LANES - HEAD_DIM // 2, 1))
                r = zt * cos + partner * sin
                c0 = 256 * j + LANES * t
                emit(r * scale, c0, out_refs[0], out_refs[1])

    a = ATTN_WIDTH
    d_hi, d_lo = DILATIONS[0], DILATIONS[1]
    head_norm_rope(proj[:, 0:a], qg_ref, (q_ref, ((d_hi, q16_ref), (d_lo, q4_ref))), HEAD_DIM ** -0.5)
    head_norm_rope(proj[:, a:2 * a], kg_ref, (k_ref, ((d_hi, k16_ref), (d_lo, k4_ref))), 1.0)
    for c0 in range(0, a, LANES):
        emit(proj[:, 2 * a + c0:2 * a + c0 + LANES], c0, v_ref, ((d_hi, v16_ref), (d_lo, v4_ref)))

    c = CONV_WIDTH
    bg = proj[:, 3 * a:3 * a + c]
    cg = proj[:, 3 * a + c:3 * a + 2 * c]
    xv = proj[:, 3 * a + 2 * c:3 * a + 3 * c]
    u = cg * xv

    @pl.when(i == 0)
    def _():
        ubuf_ref[0:SUBLANES, :] = jnp.zeros((SUBLANES, c), F32)

    ubuf_ref[SUBLANES:SUBLANES + tm, :] = u
    u1 = ubuf_ref[SUBLANES - 1:SUBLANES - 1 + tm, :]
    u2 = ubuf_ref[SUBLANES - 2:SUBLANES - 2 + tm, :]
    y = u2 * cw_ref[0:1, :] + u1 * cw_ref[1:2, :] + u * cw_ref[2:3, :]
    conv_ref[...] = (bg * y).astype(BF16)
    ubuf_ref[0:SUBLANES, :] = ubuf_ref[tm:tm + SUBLANES, :]


def _inproj(x, g, sc, sh, w_in, qg, kg, cos_t, sin_t, bd, conv_w, tm):
    s = x.shape[0]
    row = lambda n: pl.BlockSpec((1, n), lambda i: (0, 0))
    tok = lambda n: pl.BlockSpec((tm, n), lambda i: (i, 0))
    return pl.pallas_call(
        functools.partial(_inproj_kernel, tm=tm),
        grid=(s // tm,),
        in_specs=[tok(D_MODEL), row(D_MODEL), row(D_MODEL), row(D_MODEL),
                  pl.BlockSpec((D_MODEL, IN_PROJ_WIDTH), lambda i: (0, 0)),
                  row(ATTN_WIDTH), row(ATTN_WIDTH), tok(LANES), tok(LANES),
                  pl.BlockSpec((256, 256), lambda i: (0, 0)),
                  pl.BlockSpec((3, CONV_WIDTH), lambda i: (0, 0))],
        out_specs=[tok(ATTN_WIDTH)] * 3
        + [pl.BlockSpec((tm // d, d * ATTN_WIDTH), lambda i: (i, 0)) for d in DILATIONS[:2] for _ in range(3)]
        + [tok(CONV_WIDTH)],
        out_shape=[jax.ShapeDtypeStruct((s, ATTN_WIDTH), BF16)] * 3
        + [jax.ShapeDtypeStruct((s // d, d * ATTN_WIDTH), BF16) for d in DILATIONS[:2] for _ in range(3)]
        + [jax.ShapeDtypeStruct((s, CONV_WIDTH), BF16)],
        scratch_shapes=[pltpu.VMEM((tm + SUBLANES, CONV_WIDTH), F32),
                        pltpu.VMEM((3 * ATTN_WIDTH // LANES, tm, LANES), F32)],
        compiler_params=_params("arbitrary"),
        name="inproj",
    )(x, g, sc, sh, w_in, qg, kg, cos_t, sin_t, bd, conv_w)


ATTN_BLOCKS_PER_STEP = 2


def _attn_kernel(*refs, first, last):
    tq = BRANCH_KEYS
    if first:
        q_ref, kp_ref, kc_ref, vp_ref, vc_ref, acc_out, ml_out = refs
    elif last:
        q_ref, kp_ref, kc_ref, vp_ref, vc_ref, acc_in, ml_in, out_ref = refs
    else:
        q_ref, kp_ref, kc_ref, vp_ref, vc_ref, acc_in, ml_in, acc_out, ml_out = refs
    ib = pl.program_id(1)
    qi = lax.broadcasted_iota(jnp.int32, (tq, tq), 0)
    ki = lax.broadcasted_iota(jnp.int32, (tq, tq), 1)
    mask_c = ki <= qi
    mask_p = ki >= qi
    lane = ki
    probs = []
    for sub in range(ATTN_BLOCKS_PER_STEP):
        rows = slice(tq * sub, tq * (sub + 1))
        before = slice(tq * (sub - 1), tq * sub)
        for h in range(N_ATTN_HEADS):
            sl = slice(HEAD_DIM * h, HEAD_DIM * (h + 1))
            if sub == 0:
                probs.append((rows, sl, h, kp_ref[:, sl], vp_ref[:, sl], jnp.logical_and(mask_p, ib > 0)))
            else:
                probs.append((rows, sl, h, kc_ref[before, sl], vc_ref[before, sl], mask_p))
    sp = [jnp.where(mp, _dot_nt(q_ref[rows, sl], kprev), NEG_INF) for rows, sl, _, kprev, _, mp in probs]
    sc = [jnp.where(mask_c, _dot_nt(q_ref[rows, sl], kc_ref[rows, sl]), NEG_INF) for rows, sl, *_ in probs]
    m_new = [jnp.max(jnp.maximum(a, b), axis=1, keepdims=True) for a, b in zip(sp, sc)]
    if not first:
        ml_old = [ml_in[tq * sub:tq * (sub + 1), :] for sub in range(ATTN_BLOCKS_PER_STEP)]
        old = [ml_old[rows.start // tq] for rows, *_ in probs]
        m_old = [jnp.max(jnp.where(lane == p[2], o, NEG_INF), axis=1, keepdims=True) for p, o in zip(probs, old)]
        l_old = [jnp.sum(jnp.where(lane == 16 + p[2], o, 0.0), axis=1, keepdims=True) for p, o in zip(probs, old)]
        m_new = [jnp.maximum(a, b) for a, b in zip(m_old, m_new)]
        alpha = [jnp.exp(a - b) for a, b in zip(m_old, m_new)]
    pp = [jnp.exp(s - m) for s, m in zip(sp, m_new)]
    pc = [jnp.exp(s - m) for s, m in zip(sc, m_new)]
    l_new = [jnp.sum(a + b, axis=1, keepdims=True) for a, b in zip(pp, pc)]
    acc = [_dot(a.astype(BF16), p[4]) + _dot(b.astype(BF16), vc_ref[p[0], p[1]]) for a, b, p in zip(pp, pc, probs)]
    if not first:
        l_new = [a * lo + ln for a, lo, ln in zip(alpha, l_old, l_new)]
        acc = [a * acc_in[p[0], p[1]].astype(F32) + pv for a, p, pv in zip(alpha, probs, acc)]
    if last:
        for p, a, l in zip(probs, acc, l_new):
            out_ref[p[0], p[1]] = (a / l).astype(BF16)
    else:
        ml_new = [jnp.zeros((tq, LANES), F32) for _ in range(ATTN_BLOCKS_PER_STEP)]
        for p, a, m, l in zip(probs, acc, m_new, l_new):
            rows, sl, h = p[0], p[1], p[2]
            acc_out[rows, sl] = a.astype(BF16)
            sub = rows.start // tq
            ml_new[sub] = jnp.where(lane == h, m, ml_new[sub])
            ml_new[sub] = jnp.where(lane == 16 + h, l, ml_new[sub])
        for sub in range(ATTN_BLOCKS_PER_STEP):
            ml_out[tq * sub:tq * (sub + 1), :] = ml_new[sub]


def _attn_pass(q, k, v, state, dil, first, last):
    n = q.shape[0]
    s = n * dil
    tq = BRANCH_KEYS
    nb = ATTN_BLOCKS_PER_STEP
    view = lambda a: a.reshape(n, dil * a.shape[1])
    cur = lambda w: pl.BlockSpec((nb * tq, w), lambda r, ib: (ib, r))
    prev = lambda w: pl.BlockSpec((tq, w), lambda r, ib: (jnp.maximum(nb * ib - 1, 0), r))
    aw = ATTN_WIDTH
    in_specs = [cur(aw), prev(aw), cur(aw), prev(aw), cur(aw)]
    args = [q, k, k, v, v]
    if not first:
        in_specs += [cur(aw), cur(LANES)]
        args += [view(state[0]), view(state[1])]
    if last:
        out_specs = [cur(aw)]
        out_shape = [jax.ShapeDtypeStruct((n, dil * aw), BF16)]
    else:
        out_specs = [cur(aw), cur(LANES)]
        out_shape = [jax.ShapeDtypeStruct((n, dil * aw), BF16), jax.ShapeDtypeStruct((n, dil * LANES), F32)]
    outs = pl.pallas_call(
        functools.partial(_attn_kernel, first=first, last=last),
        grid=(dil, n // (nb * tq)),
        in_specs=in_specs, out_specs=out_specs, out_shape=out_shape,
        compiler_params=_params("parallel", "parallel"),
        name=f"attn_d{dil}",
    )(*args)
    return [o.reshape(s, -1) for o in outs]


def _attention(qkv_views):
    state = None
    for idx, dil in enumerate(DILATIONS):
        state = _attn_pass(*qkv_views[dil], state, dil, first=idx == 0, last=idx == len(DILATIONS) - 1)
    return state[0]


def _outproj_kernel(x_ref, attn_ref, conv_ref, woa_ref, woc_ref, gt_ref, g_ref, sc_ref, sh_ref, wq_ref,
                    keys_ref, x1_ref, h2_ref, st_ref):
    mix = _dot(attn_ref[...], woa_ref[...]) + _dot(conv_ref[...], woc_ref[...])
    x1 = x_ref[...] + gt_ref[...] * mix
    x1_ref[...] = x1
    h2 = _rms_mod(x1, g_ref[...], sc_ref[...], sh_ref[...]).astype(BF16)
    h2_ref[...] = h2
    qp = _dot(h2, wq_ref[...])
    for g in range(2 * PEER_HEADS):
        qg = qp[:, PEER_NKEYS * g:PEER_NKEYS * (g + 1)].astype(BF16)
        st_ref[PEER_NKEYS * g:PEER_NKEYS * (g + 1), :] = _dot_nt(keys_ref[g], qg)


def _outproj(x, attn, conv, wo_a, wo_c, gt, g, sc, sh, wq, keys, tm):
    s = x.shape[0]
    nq = wq.shape[1]
    row = lambda n: pl.BlockSpec((1, n), lambda i: (0, 0))
    tok = lambda n: pl.BlockSpec((tm, n), lambda i: (i, 0))
    full = lambda a: pl.BlockSpec(a.shape, lambda i: (0,) * a.ndim)
    return pl.pallas_call(
        _outproj_kernel,
        grid=(s // tm,),
        in_specs=[tok(D_MODEL), tok(ATTN_WIDTH), tok(CONV_WIDTH), full(wo_a), full(wo_c),
                  row(D_MODEL), row(D_MODEL), row(D_MODEL), row(D_MODEL), full(wq), full(keys)],
        out_specs=[tok(D_MODEL), tok(D_MODEL), pl.BlockSpec((nq, tm), lambda i: (0, i))],
        out_shape=[jax.ShapeDtypeStruct((s, D_MODEL), F32), jax.ShapeDtypeStruct((s, D_MODEL), BF16),
                   jax.ShapeDtypeStruct((nq, s), F32)],
        compiler_params=_params("parallel"),
        name="outproj",
    )(x, attn, conv, wo_a, wo_c, gt, g, sc, sh, wq, keys)


def _top16(ss, tb):
    n = ss[0].shape[0]
    row = lax.broadcasted_iota(jnp.int32, (n, tb), 0)
    row16 = lax.broadcasted_iota(jnp.int32, (PEER_TOPK, tb), 0)
    cur = list(ss)
    vals = [jnp.zeros((PEER_TOPK, tb), F32) for _ in ss]
    mark = lambda kk: -(F32_MAX - kk * F32_MAX_ULP)
    for kk in range(PEER_TOPK):
        m = [jnp.max(c, axis=0, keepdims=True) for c in cur]
        idx = [jnp.min(jnp.where(c == mm, row, n), axis=0, keepdims=True) for c, mm in zip(cur, m)]
        cur = [jnp.where(row == i, mark(kk), c) for i, c in zip(idx, cur)]
        vals = [jnp.where(row16 == kk, mm, v) for mm, v in zip(m, vals)]
    rank = [jnp.where(c <= mark(PEER_TOPK - 1), (c - mark(0)) * (1.0 / F32_MAX_ULP), float(PEER_TOPK))
            for c in cur]
    return vals, rank


def _topk_kernel(st_ref, rank2_ref, cnt_ref, e1_ref, e2_ref, *, tb):
    kk = PEER_TOPK
    s1 = st_ref[0:PEER_NKEYS, :]
    s2 = st_ref[PEER_NKEYS:2 * PEER_NKEYS, :]
    (v1, v2), (rank1, rank2) = _top16([s1, s2], tb)

    sub = lax.broadcasted_iota(jnp.int32, (SUBLANES, tb), 0)
    v2lo, v2hi = v2[0:SUBLANES, :], v2[SUBLANES:kk, :]
    groups = [(v1[0:1, :] + v2lo, sub, 0), (v1[0:1, :] + v2hi, sub + SUBLANES, 0),
              (v1[1:2, :] + v2lo, sub + kk, 1)]
    for a in range(2, SUBLANES):
        nb = kk // (a + 1)
        groups.append((jnp.where(sub < nb, v1[a:a + 1, :] + v2lo, NEG_INF), sub + kk * a, a))
    groups.append((v1[SUBLANES:kk, :] + v2[0:1, :], (sub + SUBLANES) * kk, None))
    cand = [g[0] for g in groups]
    flat = [g[1] for g in groups]
    cur = list(cand)
    big = kk * kk
    for _ in range(kk):
        m = functools.reduce(jnp.maximum, cur)
        m = jnp.max(m, axis=0, keepdims=True)
        hit = [jnp.where(c == m, f, big) for c, f in zip(cur, flat)]
        idx = jnp.min(functools.reduce(jnp.minimum, hit), axis=0, keepdims=True)
        cur = [jnp.where(f == idx, NEG_INF, c) for c, f in zip(cur, flat)]
    sel = [jnp.logical_and(c == NEG_INF, o != NEG_INF) for c, o in zip(cur, cand)]

    one = lambda mk: jnp.where(mk, 1.0, 0.0)
    counts = [jnp.sum(one(sel[0]) + one(sel[1]), axis=0, keepdims=True)]
    for gi in range(2, 2 + SUBLANES - 1):
        counts.append(jnp.sum(one(sel[gi]), axis=0, keepdims=True))
    tail = one(sel[-1])
    cnt = jnp.zeros((PEER_NKEYS, tb), F32)
    for a in range(kk):
        na = counts[a] if a < SUBLANES else tail[a - SUBLANES:a - SUBLANES + 1, :]
        cnt = jnp.where(rank1 == float(a), na, cnt)

    m1, m2 = v1[0:1, :], v2[0:1, :]
    mt = m1 + m2
    z = functools.reduce(lambda x, y: x + y,
                         [jnp.sum(jnp.where(sl, jnp.exp(o - mt), 0.0), axis=0, keepdims=True)
                          for sl, o in zip(sel, cand)])
    rank2_ref[...] = pltpu.bitcast(rank2.astype(BF16), jnp.uint32)
    cnt_ref[...] = cnt
    e1_ref[...] = jnp.exp(s1 - m1)
    e2_ref[...] = pltpu.bitcast((jnp.exp(s2 - m2) / z).astype(BF16), jnp.uint32)


def _topk(st, tb):
    s = st.shape[1]
    crow = PEER_NKEYS * SUBLANES // PACK
    plane = pl.BlockSpec((None, PEER_NKEYS, tb), lambda h, i: (h, 0, i))
    cplane = pl.BlockSpec((None, crow, tb), lambda h, i: (h, 0, i))
    shape = jax.ShapeDtypeStruct((PEER_HEADS, PEER_NKEYS, s), F32)
    cshape = jax.ShapeDtypeStruct((PEER_HEADS, crow, s), jnp.uint32)
    return pl.pallas_call(
        functools.partial(_topk_kernel, tb=tb),
        grid=(PEER_HEADS, s // tb),
        in_specs=[pl.BlockSpec((2 * PEER_NKEYS, tb), lambda h, i: (h, i))],
        out_specs=[cplane, plane, plane, cplane],
        out_shape=[cshape, shape, shape, cshape],
        compiler_params=_params("parallel", "parallel"),
        name="peer_topk",
    )(st)


PACK = 16
PEER_PIECES = 4


def _peer_kernel(h_ref, u_ref, vt_ref, rank2_ref, cnt_ref, e1_ref, e2_ref, x1_ref, gt_ref,
                 out_ref, acc_ref, p_ref, *, tm, ec):
    j = pl.program_id(1)
    rows = ec // PEER_NKEYS

    @pl.when(j == 0)
    def _():
        acc_ref[...] = jnp.zeros_like(acc_ref)

    crows = pl.ds(pl.multiple_of(j * rows, SUBLANES), rows)
    nt = tm // LANES
    tiles_per_piece = nt // PEER_PIECES
    er = ec // PEER_PIECES
    erw = er * SUBLANES // PACK
    zt = []
    for t in range(nt):
        if t % tiles_per_piece == 0:
            q = t // tiles_per_piece
            u_q = pltpu.bitcast(u_ref[erw * q:erw * (q + 1), :], BF16)
            zt.append(_dot_nt(u_q, h_ref[...]))
        tl = slice(LANES * t, LANES * (t + 1))
        cnt_rows = [cnt_ref[hh, crows, tl] for hh in range(PEER_HEADS)]
        e1_rows = [e1_ref[hh, crows, tl] for hh in range(PEER_HEADS)]
        for r in range(rows):
            w = [None] * (PEER_NKEYS // PACK)
            for hh in range(PEER_HEADS):
                cb = jnp.broadcast_to(cnt_rows[hh][r:r + 1, :], (PACK, LANES)).astype(BF16)
                eb = jnp.broadcast_to(e1_rows[hh][r:r + 1, :], (PACK, LANES)).astype(BF16)
                for g in range(PEER_NKEYS // PACK):
                    gs = slice(SUBLANES * g, SUBLANES * (g + 1))
                    rk = pltpu.bitcast(rank2_ref[hh, gs, tl], BF16)
                    e2 = pltpu.bitcast(e2_ref[hh, gs, tl], BF16)
                    term = jnp.where(rk < cb, e2, 0.0) * eb
                    w[g] = term if w[g] is None else w[g] + term
            p_ref[PEER_NKEYS * r:PEER_NKEYS * (r + 1), tl] = jnp.concatenate(w, axis=0)
    for q in range(PEER_PIECES):
        rs = slice(er * q, er * (q + 1))
        p_ref[rs, :] = p_ref[rs, :] * _gelu(zt[q].astype(BF16))
    acc_ref[...] += _dot(pltpu.bitcast(vt_ref[...], BF16), p_ref[...])

    @pl.when(j == pl.num_programs(1) - 1)
    def _():
        out_ref[...] = x1_ref[...] + gt_ref[...] * acc_ref[...].T


def _peer(h2, u, vt, planes, x1, gt, tm, ec):
    s = h2.shape[0]
    ne = vt.shape[1]
    assert ec % (SUBLANES * PEER_NKEYS) == 0 and (tm // LANES) % PEER_PIECES == 0, (ec, tm)
    plane = pl.BlockSpec((PEER_HEADS, PEER_NKEYS, tm), lambda i, j: (0, 0, i))
    cplane = pl.BlockSpec((PEER_HEADS, PEER_NKEYS * SUBLANES // PACK, tm), lambda i, j: (0, 0, i))
    tok = pl.BlockSpec((tm, D_MODEL), lambda i, j: (i, 0))
    return pl.pallas_call(
        functools.partial(_peer_kernel, tm=tm, ec=ec),
        grid=(s // tm, ne // ec),
        in_specs=[tok, pl.BlockSpec((ec * SUBLANES // PACK, D_MODEL), lambda i, j: (j, 0)),
                  pl.BlockSpec((D_MODEL * SUBLANES // PACK, ec), lambda i, j: (0, j)),
                  cplane, plane, plane, cplane, tok, pl.BlockSpec((1, D_MODEL), lambda i, j: (0, 0))],
        out_specs=tok,
        out_shape=jax.ShapeDtypeStruct((s, D_MODEL), F32),
        scratch_shapes=[pltpu.VMEM((D_MODEL, tm), F32), pltpu.VMEM((ec, tm), BF16)],
        compiler_params=_params("parallel", "arbitrary"),
        name="peer_dense",
    )(h2, u, vt, *planes, x1, gt)


def _words_kernel(x_ref, o_ref, *, transpose):
    x = x_ref[...]
    if transpose:
        x = x.T
    o_ref[...] = pltpu.bitcast(x.astype(BF16), jnp.uint32)


def _expert_table_words(tabs, layer, transpose):
    _, ne, d = tabs.shape
    blk = D_MODEL
    shrink = PACK // SUBLANES
    if transpose:
        out_shape, out_spec = (d // shrink, ne), pl.BlockSpec((d // shrink, blk), lambda i: (0, i))
    else:
        out_shape, out_spec = (ne // shrink, d), pl.BlockSpec((blk // shrink, d), lambda i: (i, 0))
    return pl.pallas_call(
        functools.partial(_words_kernel, transpose=transpose),
        grid=(ne // blk,),
        in_specs=[pl.BlockSpec((None, blk, d), lambda i: (layer, i, 0))],
        out_specs=out_spec,
        out_shape=jax.ShapeDtypeStruct(out_shape, jnp.uint32),
        compiler_params=_params("parallel"),
        name="expert_words_t" if transpose else "expert_words",
    )(tabs)


def _tile(s, want):
    t = min(want, s)
    assert s % t == 0, (s, t)
    return t


def _layer(x, mod, cos_t, sin_t, bd, norm_mix, norm_ffn, w_in, q_norm, k_norm, conv_w, w_out,
           peer_wq, peer_keys, peer_u_all, peer_v_all, layer):
    s = x.shape[0]
    sh1, sc1, gt1, sh2, sc2, gt2 = [m.reshape(1, D_MODEL) for m in jnp.split(mod, N_MOD)]
    tm = _tile(s, 512)
    qg = jnp.tile(q_norm, N_ATTN_HEADS).reshape(1, ATTN_WIDTH)
    kg = jnp.tile(k_norm, N_ATTN_HEADS).reshape(1, ATTN_WIDTH)
    *qkv, conv = _inproj(x, norm_mix.reshape(1, D_MODEL), sc1, sh1, w_in.astype(BF16), qg, kg,
                         cos_t, sin_t, bd, conv_w, tm)
    views = {1: qkv[0:3], DILATIONS[0]: qkv[3:6], DILATIONS[1]: qkv[6:9]}
    attn = _attention(views)
    wo = w_out.astype(BF16)
    keys = peer_keys.reshape(2 * PEER_HEADS, PEER_NKEYS, -1).astype(BF16)
    x1, h2, st = _outproj(x, attn, conv, wo[:ATTN_WIDTH], wo[ATTN_WIDTH:], gt1,
                          norm_ffn.reshape(1, D_MODEL), sc2, sh2, peer_wq.astype(BF16), keys, tm)
    planes = _topk(st, _tile(s, 1024))
    return _peer(h2, _expert_table_words(peer_u_all, layer, False), _expert_table_words(peer_v_all, layer, True),
                 planes, x1, gt2, _tile(s, 512), 2 * SUBLANES * PEER_NKEYS)


def kernel(x, c, positions, w_ada, b_ada, norm_mix, norm_ffn, w_in, q_norm, k_norm, conv_w, w_out,
           peer_wq, peer_keys, peer_u, peer_v):
    b, s, d = x.shape
    assert b == 1 and d == D_MODEL and s % (max(DILATIONS) * BRANCH_KEYS * ATTN_BLOCKS_PER_STEP) == 0, x.shape
    depth = w_ada.shape[0]
    mod = _ada(c, w_ada, b_ada)
    cos_t, sin_t = _rope_tables(positions, _tile(s, 1024))
    blk = jnp.arange(256) // HEAD_DIM
    bd = (blk[:, None] == blk[None, :]).astype(BF16)
    y = x.reshape(s, d)
    for l in range(depth):
        y = _layer(y, mod[l], cos_t, sin_t, bd, norm_mix[l], norm_ffn[l], w_in[l], q_norm[l], k_norm[l],
                   conv_w[l], w_out[l], peer_wq[l], peer_keys[l], peer_u, peer_v, l)
    return y.reshape(b, s, d)
```
